```python
import jax, jax.numpy as jnp
from jax import lax
import numpy as np

D_MODEL = 1024
BATCH = 16
SEQ = 256
DEPTH = 4
DEC_BATCH = 4
DEC_SEQ = 2048
PAST_LEN = 256

GRID_W = 64
D_RWKV = 512
HEAD_SIZE = 64
N_HEADS_RWKV = D_RWKV // HEAD_SIZE
D_DECAY_LORA = 64
D_ICLR_LORA = 64
D_GATE_LORA = 128
N_DIR = 2
GN_EPS = 64e-5
D_POOL = D_MODEL - D_RWKV
POOL_WINDOWS = (2, 4, 8, 16)
N_POOL_GROUPS = len(POOL_WINDOWS)
POOL_GROUP = D_POOL // N_POOL_GROUPS
N_EXPERTS = 16
CAPACITY_FACTOR = 2
D_EXPERT = 2 * D_MODEL
N_MOD = 6
RMS_EPS = 1e-6
OFF_K = D_RWKV
OFF_V = 2 * D_RWKV
OFF_G = 3 * D_RWKV
OFF_W = OFF_G + D_GATE_LORA
OFF_A = OFF_W + N_DIR * D_DECAY_LORA
OFF_POOL = OFF_A + N_DIR * D_ICLR_LORA
D_IN = OFF_POOL + D_POOL

kernel_name = 'hybrid_rwkv7_pool_ecmoe_diffusion_step'


def rms_norm(x, gain):
    x32 = x.astype(jnp.float32)
    y = x32 * lax.rsqrt(jnp.mean(x32 * x32, axis=-1, keepdims=True) + RMS_EPS)
    return (y * gain.astype(jnp.float32)).astype(x.dtype)


def centred_shift(z):
    prev = jnp.pad(z[:, :-1], ((0, 0), (1, 0), (0, 0)))
    nxt = jnp.pad(z[:, 1:], ((0, 0), (0, 1), (0, 0)))
    return 0.5 * (prev + nxt) - z


def box_sum(z, window, axis):
    n = z.shape[axis]
    pad = [(0, 0)] * z.ndim
    pad[axis] = (1, 0)
    cs = jnp.pad(jnp.cumsum(z, axis=axis), pad)
    lo = window // 2
    hi = window - 1 - lo
    pos = np.arange(n)
    start = np.clip(pos - lo, 0, n)
    end = np.clip(pos + hi + 1, 0, n)
    s = jnp.take(cs, end, axis=axis) - jnp.take(cs, start, axis=axis)
    return s, (end - start).astype(np.float32)


def multiscale_pool(z, pool_w, pool_scale, use_grid):
    b, t, _ = z.shape
    z32 = z.astype(jnp.float32).reshape(b, t, N_POOL_GROUPS, POOL_GROUP)
    outs = []
    for gi, win in enumerate(POOL_WINDOWS):
        zg = z32[:, :, gi]
        if use_grid:
            rows = t // GRID_W
            s, cnt_c = box_sum(zg.reshape(b, rows, GRID_W, POOL_GROUP), win, 2)
            s, cnt_r = box_sum(s, win, 1)
            cnt = (cnt_r[:, None] * cnt_c[None, :])[None, :, :, None]
            mean = (s / cnt).reshape(b, t, POOL_GROUP)
        else:
            s, cnt = box_sum(zg, win, 1)
            mean = s / cnt[None, :, None]
        outs.append(mean - zg)
    d = jnp.stack(outs, axis=2).astype(z.dtype)
    y = jnp.einsum('btgc,gcd->btgd', d, pool_w).reshape(b, t, D_POOL)
    return y * pool_scale


def wkv_scan(s0, r, w, k, v, kk, kka, reverse):
    xs = tuple(jnp.moveaxis(u, 1, 0) for u in (r, w, k, v, kk, kka))

    def step(s, inp):
        r_t, w_t, k_t, v_t, kk_t, b_t = inp
        sa = jnp.einsum('bhvk,bhk->bhv', s, kk_t)
        s = (s * w_t[:, :, None, :] - sa[..., None] * b_t[:, :, None, :]
             + v_t[..., None] * k_t[:, :, None, :])
        y = jnp.einsum('bhvk,bhk->bhv', s, r_t)
        return s, y

    s_fin, ys = lax.scan(step, s0.astype(jnp.float32), xs, reverse=reverse)
    return s_fin, jnp.moveaxis(ys, 0, 1)


def rwkv7_bidir(p, s0, mu_shift, w0, w_up, a0, a_up, g_up, k_k, k_a, r_k, ln_x_w, ln_x_b):
    b, t, _ = p.shape
    z = p[..., :OFF_POOL]
    z = (z + mu_shift * centred_shift(z)).astype(jnp.float32)
    r = z[..., :OFF_K]
    k = z[..., OFF_K:OFF_V]
    v = z[..., OFF_V:OFF_G]
    gd = z[..., OFF_G:OFF_W]
    wd = z[..., OFF_W:OFF_A].reshape(b, t, N_DIR, D_DECAY_LORA)
    ad = z[..., OFF_A:OFF_POOL].reshape(b, t, N_DIR, D_ICLR_LORA)
    w_log = -jax.nn.softplus(-(w0 + jnp.einsum('btdr,drc->btdc', jnp.tanh(wd), w_up))) - 0.5
    decay = jnp.exp(-jnp.exp(w_log))
    a = jax.nn.sigmoid(a0 + jnp.einsum('btdr,drc->btdc', ad, a_up))
    g = jnp.einsum('btr,rc->btc', jax.nn.sigmoid(gd), g_up)

    def heads(u):
        return u.reshape(u.shape[:-1] + (N_HEADS_RWKV, HEAD_SIZE))

    kk = heads(k * k_k)
    kk = kk / jnp.maximum(jnp.sqrt(jnp.sum(kk * kk, axis=-1, keepdims=True)), 1e-12)
    k_dir = heads(k[:, :, None, :] * (1.0 + (a - 1.0) * k_a))
    rh, vh, ah, wh = heads(r), heads(v), heads(a), heads(decay)
    s_f, y_f = wkv_scan(s0[:, 0], rh, wh[:, :, 0], k_dir[:, :, 0], vh, kk, kk * ah[:, :, 0], False)
    s_b, y_b = wkv_scan(s0[:, 1], rh, wh[:, :, 1], k_dir[:, :, 1], vh, kk, kk * ah[:, :, 1], True)
    y = y_f + y_b
    mu = jnp.mean(y, axis=-1, keepdims=True)
    var = jnp.mean(jnp.square(y - mu), axis=-1, keepdims=True)
    y = ((y - mu) * lax.rsqrt(var + GN_EPS)).reshape(b, t, D_RWKV) * ln_x_w + ln_x_b
    bonus = jnp.einsum('bthn,btdhn,hn->bth', rh, k_dir,
                       r_k.reshape(N_HEADS_RWKV, HEAD_SIZE).astype(jnp.float32))[..., None] * vh
    y = (y + bonus.reshape(b, t, D_RWKV)) * g
    return y.astype(p.dtype), jnp.stack([s_f, s_b], axis=1)


def expert_choice_ffn(h, w_router, w1, w3, w2):
    bn, t, d = h.shape
    hf = h.reshape(-1, d)
    n_tok = hf.shape[0]
    cap = CAPACITY_FACTOR * n_tok // N_EXPERTS
    probs = jax.nn.softmax((hf @ w_router).astype(jnp.float32), axis=-1)
    gate, idx = lax.top_k(probs.T, cap)
    xe = hf[idx]
    he = jax.nn.silu(jnp.einsum('ecd,edf->ecf', xe, w1)) * jnp.einsum('ecd,edf->ecf', xe, w3)
    ye = jnp.einsum('ecf,efd->ecd', he, w2) * gate[..., None].astype(h.dtype)
    out = jnp.zeros((n_tok, d), ye.dtype).at[idx.reshape(-1)].add(ye.reshape(-1, d))
    return out.reshape(bn, t, d).astype(h.dtype)


def trunk_layer(x, cvec, s0, use_grid, norm1, norm2, ada_w, ada_b, w_in, mu_shift, w0, w_up,
                a0, a_up, g_up, k_k, k_a, r_k, ln_x_w, ln_x_b, pool_w, pool_scale, w_out,
                w_router, w1, w3, w2):
    mod = jnp.einsum('bd,de->be', jax.nn.silu(cvec), ada_w) + ada_b
    sh1, sc1, gt1, sh2, sc2, gt2 = jnp.split(mod[:, None, :], N_MOD, axis=-1)
    h = rms_norm(x, norm1) * (1.0 + sc1) + sh1
    p = h @ w_in
    y_rwkv, s_fin = rwkv7_bidir(p, s0, mu_shift, w0, w_up, a0, a_up, g_up, k_k, k_a, r_k,
                                ln_x_w, ln_x_b)
    y_pool = multiscale_pool(p[..., OFF_POOL:], pool_w, pool_scale, use_grid)
    m = jnp.concatenate([y_rwkv, y_pool.astype(y_rwkv.dtype)], axis=-1) @ w_out
    x = x + gt1 * m
    h = rms_norm(x, norm2) * (1.0 + sc2) + sh2
    x = x + gt2 * expert_choice_ffn(h, w_router, w1, w3, w2)
    return x, s_fin


def setup_inputs(seed: int = 0) -> dict:
    key = jax.random.key(seed)
    ks = jax.random.split(key, 32)
    f32 = jnp.float32
    L = DEPTH
    D = D_MODEL

    def nrm(k, shape, s):
        return jax.random.normal(k, shape, f32) * s

    return {
        'x_prompt': nrm(ks[0], (BATCH, SEQ, D), 1.0),
        'x_sample': nrm(ks[1], (DEC_BATCH, DEC_SEQ, D), 1.0),
        'state_rwkv': nrm(ks[2], (DEC_BATCH, L, N_DIR, N_HEADS_RWKV, HEAD_SIZE, HEAD_SIZE), 0.3),
        'c': nrm(ks[3], (DEC_BATCH, D), 1.0),
        'c_ctx': nrm(ks[4], (D,), 1.0),
        'norm1': 1.0 + nrm(ks[5], (L, D), 0.05),
        'norm2': 1.0 + nrm(ks[6], (L, D), 0.05),
        'norm_f': 1.0 + nrm(ks[7], (D,), 0.05),
        'ada_w': nrm(ks[8], (L, D, N_MOD * D), 0.5 * D ** -0.5),
        'ada_b': nrm(ks[9], (L, N_MOD * D), 0.05),
        'w_in': nrm(ks[10], (L, D, D_IN), D ** -0.5),
        'mu_shift': 0.5 + nrm(ks[11], (L, OFF_POOL), 0.1),
        'w0': -1.0 + nrm(ks[12], (L, N_DIR, D_RWKV), 0.5),
        'w_up': nrm(ks[13], (L, N_DIR, D_DECAY_LORA, D_RWKV), 0.1 * D_DECAY_LORA ** -0.5),
        'a0': nrm(ks[14], (L, N_DIR, D_RWKV), 0.5),
        'a_up': nrm(ks[15], (L, N_DIR, D_ICLR_LORA, D_RWKV), 0.1 * D_ICLR_LORA ** -0.5),
        'g_up': nrm(ks[16], (L, D_GATE_LORA, D_RWKV), D_GATE_LORA ** -0.5),
        'k_k': 0.85 + nrm(ks[17], (L, D_RWKV), 0.05),
        'k_a': 1.0 + nrm(ks[18], (L, D_RWKV), 0.05),
        'r_k': nrm(ks[19], (L, D_RWKV), 0.1),
        'ln_x_w': 1.0 + nrm(ks[20], (L, D_RWKV), 0.05),
        'ln_x_b': nrm(ks[21], (L, D_RWKV), 0.02),
        'pool_w': nrm(ks[22], (L, N_POOL_GROUPS, POOL_GROUP, POOL_GROUP), POOL_GROUP ** -0.5),
        'pool_scale': 1.0 + nrm(ks[23], (L, D_POOL), 0.05),
        'w_out': nrm(ks[24], (L, D, D), D ** -0.5),
        'w_router': nrm(ks[25], (L, D, N_EXPERTS), D ** -0.5),
        'w1': nrm(ks[26], (L, N_EXPERTS, D, D_EXPERT), D ** -0.5),
        'w3': nrm(ks[27], (L, N_EXPERTS, D, D_EXPERT), D ** -0.5),
        'w2': nrm(ks[28], (L, N_EXPERTS, D_EXPERT, D), D_EXPERT ** -0.5),
    }


def reference(x_prompt, x_sample, state_rwkv, c, c_ctx, norm1, norm2, norm_f, ada_w, ada_b,
              w_in, mu_shift, w0, w_up, a0, a_up, g_up, k_k, k_a, r_k, ln_x_w, ln_x_b,
              pool_w, pool_scale, w_out, w_router, w1, w3, w2):
    def run_layer(x, cvec, s0, use_grid, l):
        return trunk_layer(x, cvec, s0, use_grid, norm1[l], norm2[l], ada_w[l], ada_b[l],
                           w_in[l], mu_shift[l], w0[l], w_up[l], a0[l], a_up[l], g_up[l],
                           k_k[l], k_a[l], r_k[l], ln_x_w[l], ln_x_b[l], pool_w[l],
                           pool_scale[l], w_out[l], w_router[l], w1[l], w3[l], w2[l])

    b_ctx = x_prompt.shape[0]
    zero_state = jnp.zeros((b_ctx, N_DIR, N_HEADS_RWKV, HEAD_SIZE, HEAD_SIZE), jnp.float32)
    x = x_prompt
    ctx_states = []
    for l in range(DEPTH):
        x, s_fin = run_layer(x, c_ctx[None, :], zero_state, False, l)
        ctx_states.append(s_fin)
    y_prompt = rms_norm(x, norm_f)
    new_state_rwkv = jnp.stack(ctx_states, axis=1)

    x = x_sample
    for l in range(DEPTH):
        x, _ = run_layer(x, c, state_rwkv[:, l], True, l)
    y_sample = rms_norm(x, norm_f)

    return (y_prompt, y_sample, new_state_rwkv)
```

```python
import functools

import jax
import jax.numpy as jnp
import numpy as np
from jax import lax
from jax.experimental import pallas as pl
from jax.experimental.pallas import tpu as pltpu

F32 = jnp.float32
BF16 = jnp.bfloat16

D = 1024
DEPTH = 4
GRID_W = 64
D_RWKV = 512
HEAD = 64
N_HEADS = 8
POOL_WINDOWS = (2, 4, 8, 16)
POOL_GROUP = 128
N_EXPERTS = 16
D_EXPERT = 2048
N_MOD = 6
RMS_EPS = 1e-6
GN_EPS = 64e-5
OFF_K = 512
OFF_V = 1024
OFF_G = 1536
OFF_W = 1664
OFF_A = 1792
OFF_POOL = 1920
D_IN = 2432

VMEM_LIMIT = 52 * 1024 * 1024
TM = 256
TB = 32
LANES = 128


def _cparams(*sem):
    return pltpu.CompilerParams(dimension_semantics=sem, vmem_limit_bytes=VMEM_LIMIT)


def _sigmoid(x):
    return 1.0 / (1.0 + jnp.exp(-x))


def _dot(a, b):
    return jnp.dot(a, b, preferred_element_type=F32)


def _split_dot(x, m):
    hi = x.astype(BF16)
    lo = (x - hi.astype(F32)).astype(BF16)
    return _dot(hi, m) + _dot(lo, m)


def _split_dot_left(m, x):
    hi = x.astype(BF16)
    lo = (x - hi.astype(F32)).astype(BF16)
    return _dot(m, hi) + _dot(m, lo)


def _mod_kernel(cv_ref, w_ref, b_ref, o_ref):
    cv = cv_ref[...]
    s = (cv * _sigmoid(cv)).astype(BF16)
    o_ref[0] = _dot(s, w_ref[0].astype(BF16)) + b_ref[0]


def _mod_call(cv, ada_w, ada_b):
    tn = 1536
    n_out = N_MOD * D
    return pl.pallas_call(
        _mod_kernel,
        grid=(DEPTH, n_out // tn),
        in_specs=[
            pl.BlockSpec((8, D), lambda l, j: (0, 0)),
            pl.BlockSpec((1, D, tn), lambda l, j: (l, 0, j)),
            pl.BlockSpec((1, 1, tn), lambda l, j: (l, 0, j)),
        ],
        out_specs=pl.BlockSpec((1, 8, tn), lambda l, j: (l, 0, j)),
        out_shape=jax.ShapeDtypeStruct((DEPTH, 8, n_out), F32),
        compiler_params=_cparams("arbitrary", "arbitrary"),
        name="adaln_mod",
    )(cv, ada_w, ada_b.reshape(DEPTH, 1, n_out))


def _in_kernel(x_ref, xp_ref, xn_ref, mod_ref, g1_ref, win_ref, mu_ref, w0_ref, wup_ref, a0_ref,
               aup_ref, gup_ref, kk_ref, ka_ref, bd_ref,
               r_o, v_o, kkn_o, g_o, w_o, kx_o, b_o, zp_o, *, tiles_per_seq):
    mod = mod_ref[0]
    sh1 = mod[:, 0:D]
    sc1 = mod[:, D:2 * D]
    g1 = g1_ref[...]

    def hfun(x):
        y = x * lax.rsqrt(jnp.mean(x * x, axis=-1, keepdims=True) + RMS_EPS)
        return ((y * g1) * (1.0 + sc1) + sh1).astype(BF16)

    p = _dot(hfun(x_ref[...]), win_ref[...])
    z = p[:, :OFF_POOL]
    rows = lax.broadcasted_iota(jnp.int32, z.shape, 0)
    prev = pltpu.roll(z, 1, 0)
    nxt = pltpu.roll(z, TM - 1, 0)
    if tiles_per_seq > 1:
        j = pl.program_id(0) % tiles_per_seq
        wz = win_ref[:, :OFF_POOL]
        pp = _dot(hfun(xp_ref[...]), wz)
        pn = _dot(hfun(xn_ref[...]), wz)
        prow = jnp.where(j > 0, pp[7:8, :], 0.0)
        nrow = jnp.where(j < tiles_per_seq - 1, pn[0:1, :], 0.0)
    else:
        prow = 0.0
        nrow = 0.0
    prev = jnp.where(rows == 0, prow, prev)
    nxt = jnp.where(rows == TM - 1, nrow, nxt)
    zs = z + mu_ref[...] * (0.5 * (prev + nxt) - z)

    r = zs[:, :OFF_K]
    k = zs[:, OFF_K:OFF_V]
    v = zs[:, OFF_V:OFF_G]
    gd = zs[:, OFF_G:OFF_W]
    wd = zs[:, OFF_W:OFF_A]
    ad = zs[:, OFF_A:OFF_POOL]

    wl = w0_ref[...] + _dot(jnp.tanh(wd).astype(BF16), wup_ref[...])
    decay = jnp.exp(-np.float32(np.exp(-0.5)) * _sigmoid(wl))
    a = _sigmoid(a0_ref[...] + _dot(ad.astype(BF16), aup_ref[...]))
    g = _dot(_sigmoid(gd).astype(BF16), gup_ref[...])

    kkr = k * kk_ref[...]
    ssq = _split_dot(kkr * kkr, bd_ref[...])
    kkn = kkr / jnp.maximum(jnp.sqrt(ssq), 1e-12)

    r_o[...] = r
    v_o[...] = v
    kkn_o[...] = kkn
    g_o[...] = g
    zp_o[...] = p[:, OFF_POOL:]
    ka = ka_ref[...]
    for d in range(2):
        a_d = a[:, d * D_RWKV:(d + 1) * D_RWKV]
        w_o[d] = decay[:, d * D_RWKV:(d + 1) * D_RWKV]
        kx_o[d] = k * (1.0 + (a_d - 1.0) * ka)
        b_o[d] = kkn * a_d


def _in_call(x, mod3, mod_row, tiles_per_seq, g1, win, mu, w0, wup, a0, aup, gup, kk_k, k_a, bd):
    n = x.shape[0]
    nt = n // TM
    nb8 = n // 8
    const = lambda i: (0, 0)
    tok = lambda i: (i, 0)
    tok3 = lambda i: (0, i, 0)
    f512 = jax.ShapeDtypeStruct((n, D_RWKV), F32)
    f2 = jax.ShapeDtypeStruct((2, n, D_RWKV), F32)
    return pl.pallas_call(
        functools.partial(_in_kernel, tiles_per_seq=tiles_per_seq),
        grid=(nt,),
        in_specs=[
            pl.BlockSpec((TM, D), tok),
            pl.BlockSpec((8, D), lambda i: (jnp.maximum(i * (TM // 8) - 1, 0), 0)),
            pl.BlockSpec((8, D), lambda i: (jnp.minimum((i + 1) * (TM // 8), nb8 - 1), 0)),
            pl.BlockSpec((1, 1, N_MOD * D), lambda i: (mod_row(i), 0, 0)),
            pl.BlockSpec((1, D), const),
            pl.BlockSpec((D, D_IN), const),
            pl.BlockSpec((1, OFF_POOL), const),
            pl.BlockSpec((1, 2 * D_RWKV), const),
            pl.BlockSpec((128, 2 * D_RWKV), const),
            pl.BlockSpec((1, 2 * D_RWKV), const),
            pl.BlockSpec((128, 2 * D_RWKV), const),
            pl.BlockSpec((128, D_RWKV), const),
            pl.BlockSpec((1, D_RWKV), const),
            pl.BlockSpec((1, D_RWKV), const),
            pl.BlockSpec((D_RWKV, D_RWKV), const),
        ],
        out_specs=[
            pl.BlockSpec((TM, D_RWKV), tok),
            pl.BlockSpec((TM, D_RWKV), tok),
            pl.BlockSpec((TM, D_RWKV), tok),
            pl.BlockSpec((TM, D_RWKV), tok),
            pl.BlockSpec((2, TM, D_RWKV), tok3),
            pl.BlockSpec((2, TM, D_RWKV), tok3),
            pl.BlockSpec((2, TM, D_RWKV), tok3),
            pl.BlockSpec((TM, D_RWKV), tok),
        ],
        out_shape=[f512, f512, f512, f512, f2, f2, f2, f512],
        compiler_params=_cparams("arbitrary"),
        name="in_proj",
    )(x, x, x, mod3, g1, win, mu, w0, wup, a0, aup, gup, kk_k, k_a, bd)


def _scan_kernel(kk_ref, w_ref, b_ref, kx_ref, r_ref, v_ref, s0_ref, y_ref, sfin_ref, s_ref, *, vp):
    tb = pl.program_id(1)

    @pl.when(tb == 0)
    def _():
        s_ref[...] = s0_ref[0]

    def step(t, carry):
        sa0 = jnp.zeros((vp, LANES), F32)
        sa1 = jnp.zeros((vp, LANES), F32)
        for k in range(0, HEAD, 2):
            sa0 = sa0 + s_ref[k] * kk_ref[0, t, pl.ds(k, 1), :]
            sa1 = sa1 + s_ref[k + 1] * kk_ref[0, t, pl.ds(k + 1, 1), :]
        sa = sa0 + sa1
        vv = v_ref[0, t]
        y0 = jnp.zeros((vp, LANES), F32)
        y1 = jnp.zeros((vp, LANES), F32)
        for k in range(HEAD):
            sn = (s_ref[k] * w_ref[0, t, pl.ds(k, 1), :] - sa * b_ref[0, t, pl.ds(k, 1), :]
                  + vv * kx_ref[0, t, pl.ds(k, 1), :])
            s_ref[k] = sn
            if k % 2 == 0:
                y0 = y0 + sn * r_ref[0, t, pl.ds(k, 1), :]
            else:
                y1 = y1 + sn * r_ref[0, t, pl.ds(k, 1), :]
        y_ref[0, t] = y0 + y1
        return carry

    lax.fori_loop(0, TB, step, 0)

    @pl.when(tb == pl.num_programs(1) - 1)
    def _():
        sfin_ref[0] = s_ref[...]


def _scan_call(kk, w, b, kx, r, v, s0):
    g, t, _, _ = kk.shape
    vp = v.shape[2]
    kspec = pl.BlockSpec((1, TB, HEAD, LANES), lambda gi, ti: (gi, ti, 0, 0))
    vspec = pl.BlockSpec((1, TB, vp, LANES), lambda gi, ti: (gi, ti, 0, 0))
    sspec = pl.BlockSpec((1, HEAD, vp, LANES), lambda gi, ti: (gi, 0, 0, 0))
    return pl.pallas_call(
        functools.partial(_scan_kernel, vp=vp),
        grid=(g, t // TB),
        in_specs=[kspec, kspec, kspec, kspec, kspec, vspec, sspec],
        out_specs=[vspec, sspec],
        out_shape=[jax.ShapeDtypeStruct((g, t, vp, LANES), F32),
                   jax.ShapeDtypeStruct((g, HEAD, vp, LANES), F32)],
        scratch_shapes=[pltpu.VMEM((HEAD, vp, LANES), F32)],
        compiler_params=_cparams("arbitrary", "arbitrary"),
        name="wkv_scan",
    )(kk, w, b, kx, r, v, s0)


def _pool_seq_kernel(z_ref, a_ref, inv_ref, pw_ref, ps_ref, o_ref):
    for gi in range(len(POOL_WINDOWS)):
        sl = slice(gi * POOL_GROUP, (gi + 1) * POOL_GROUP)
        zg = z_ref[:, sl]
        s = _split_dot_left(a_ref[gi], zg)
        d = s * inv_ref[:, sl] - zg
        o_ref[:, sl] = _dot(d.astype(BF16), pw_ref[gi]) * ps_ref[:, sl]


def _pool_seq_call(z, t, amat, inv, pw, ps):
    n = z.shape[0]
    ng = len(POOL_WINDOWS)
    return pl.pallas_call(
        _pool_seq_kernel,
        grid=(n // t,),
        in_specs=[
            pl.BlockSpec((t, D_RWKV), lambda i: (i, 0)),
            pl.BlockSpec((ng, t, t), lambda i: (0, 0, 0)),
            pl.BlockSpec((t, D_RWKV), lambda i: (0, 0)),
            pl.BlockSpec((ng, POOL_GROUP, POOL_GROUP), lambda i: (0, 0, 0)),
            pl.BlockSpec((1, D_RWKV), lambda i: (0, 0)),
        ],
        out_specs=pl.BlockSpec((t, D_RWKV), lambda i: (i, 0)),
        out_shape=jax.ShapeDtypeStruct((n, D_RWKV), F32),
        compiler_params=_cparams("arbitrary"),
        name="pool_seq",
    )(z, amat, inv, pw, ps)


POOL_BLK = 256
POOL_PAD = (max(POOL_WINDOWS) // 2) * GRID_W


def _pool_grid_kernel(z_ref, a_ref, inv_ref, pw_ref, ps_ref, o_ref, pad_ref, *, t):
    zeros = jnp.zeros((POOL_PAD, POOL_GROUP), F32)
    pad_ref[0:POOL_PAD, :] = zeros
    pad_ref[POOL_PAD + t:POOL_PAD + t + POOL_PAD, :] = zeros
    for gi, win in enumerate(POOL_WINDOWS):
        sl = slice(gi * POOL_GROUP, (gi + 1) * POOL_GROUP)
        for blk in range(t // POOL_BLK):
            zb = z_ref[blk * POOL_BLK:(blk + 1) * POOL_BLK, sl]
            pad_ref[POOL_PAD + blk * POOL_BLK:POOL_PAD + (blk + 1) * POOL_BLK, :] = (
                _split_dot_left(a_ref[gi], zb))
        lo = win // 2
        hi = win - 1 - lo
        s = pad_ref[POOL_PAD - lo * GRID_W:POOL_PAD - lo * GRID_W + t, :]
        for j in range(-lo + 1, hi + 1):
            s = s + pad_ref[POOL_PAD + j * GRID_W:POOL_PAD + j * GRID_W + t, :]
        d = s * inv_ref[:, sl] - z_ref[:, sl]
        o_ref[:, sl] = _dot(d.astype(BF16), pw_ref[gi]) * ps_ref[:, sl]


def _pool_grid_call(z, t, amat, inv, pw, ps):
    n = z.shape[0]
    ng = len(POOL_WINDOWS)
    return pl.pallas_call(
        functools.partial(_pool_grid_kernel, t=t),
        grid=(n // t,),
        in_specs=[
            pl.BlockSpec((t, D_RWKV), lambda i: (i, 0)),
            pl.BlockSpec((ng, POOL_BLK, POOL_BLK), lambda i: (0, 0, 0)),
            pl.BlockSpec((t, D_RWKV), lambda i: (0, 0)),
            pl.BlockSpec((ng, POOL_GROUP, POOL_GROUP), lambda i: (0, 0, 0)),
            pl.BlockSpec((1, D_RWKV), lambda i: (0, 0)),
        ],
        out_specs=pl.BlockSpec((t, D_RWKV), lambda i: (i, 0)),
        out_shape=jax.ShapeDtypeStruct((n, D_RWKV), F32),
        scratch_shapes=[pltpu.VMEM((t + 2 * POOL_PAD, POOL_GROUP), F32)],
        compiler_params=_cparams("arbitrary"),
        name="pool_grid",
    )(z, amat, inv, pw, ps)


def _band(n, win):
    lo = win // 2
    hi = win - 1 - lo
    pos = np.arange(n)
    start = np.clip(pos - lo, 0, n)
    end = np.clip(pos + hi + 1, 0, n)
    m = (pos[None, :] >= start[:, None]) & (pos[None, :] < end[:, None])
    return m.astype(np.float32), (end - start).astype(np.float32)


def _pool_constants(t, use_grid):
    mats, invs = [], []
    for win in POOL_WINDOWS:
        if use_grid:
            rows = t // GRID_W
            mcol, ccol = _band(GRID_W, win)
            _, crow = _band(rows, win)
            mats.append(np.kron(np.eye(POOL_BLK // GRID_W, dtype=np.float32), mcol))
            cnt = (crow[:, None] * ccol[None, :]).reshape(t)
        else:
            m, cnt = _band(t, win)
            mats.append(m)
        invs.append(np.repeat((1.0 / cnt)[:, None], POOL_GROUP, axis=1))
    return (jnp.asarray(np.stack(mats), BF16),
            jnp.asarray(np.concatenate(invs, axis=1), F32))


def _post_kernel(x_ref, yf_ref, yb_ref, r_ref, v_ref, g_ref, kx_ref, yp_ref, mod_ref, lnw_ref,
                 lnb_ref, rk_ref, bd_ref, wout_ref, g2_ref, wrh_ref, wrl_ref,
                 x1_o, h2_o, pt_o):
    mod = mod_ref[0]
    gt1 = mod[:, 2 * D:3 * D]
    sh2 = mod[:, 3 * D:4 * D]
    sc2 = mod[:, 4 * D:5 * D]
    bd = bd_ref[...]
    inv_n = np.float32(1.0 / HEAD)

    y = yf_ref[...] + yb_ref[...]
    mu = _split_dot(y, bd) * inv_n
    yc = y - mu
    var = _split_dot(yc * yc, bd) * inv_n
    yn = yc * lax.rsqrt(var + GN_EPS) * lnw_ref[...] + lnb_ref[...]
    v = v_ref[...]
    bonus = _split_dot(r_ref[...] * (kx_ref[0] + kx_ref[1]) * rk_ref[...], bd) * v
    yr = ((yn + bonus) * g_ref[...]).astype(BF16)
    m = _dot(yr, wout_ref[0:D_RWKV, :]) + _dot(yp_ref[...].astype(BF16), wout_ref[D_RWKV:D, :])
    x1 = x_ref[...] + gt1 * m
    x1_o[...] = x1
    h = x1 * lax.rsqrt(jnp.mean(x1 * x1, axis=-1, keepdims=True) + RMS_EPS)
    h2 = (h * g2_ref[...]) * (1.0 + sc2) + sh2
    h2_o[...] = h2.astype(BF16)

    hi = h2.astype(BF16)
    lo = (h2 - hi.astype(F32)).astype(BF16)
    dn = (((1,), (1,)), ((), ()))
    logits = (lax.dot_general(wrh_ref[...], hi, dn, preferred_element_type=F32)
              + lax.dot_general(wrh_ref[...], lo, dn, preferred_element_type=F32)
              + lax.dot_general(wrl_ref[...], hi, dn, preferred_element_type=F32))
    mx = jnp.max(logits, axis=0, keepdims=True)
    e = jnp.exp(logits - mx)
    pt_o[...] = e / jnp.sum(e, axis=0, keepdims=True)


def _post_call(x, yf, yb, r, v, g, kx, yp, mod3, mod_row, lnw, lnb, rk, bd, wout, g2, wrh, wrl):
    n = x.shape[0]
    const = lambda i: (0, 0)
    tok = lambda i: (i, 0)
    t512 = pl.BlockSpec((TM, D_RWKV), tok)
    v512 = pl.BlockSpec((1, D_RWKV), const)
    return pl.pallas_call(
        _post_kernel,
        grid=(n // TM,),
        in_specs=[
            pl.BlockSpec((TM, D), tok), t512, t512, t512, t512, t512,
            pl.BlockSpec((2, TM, D_RWKV), lambda i: (0, i, 0)),
            t512,
            pl.BlockSpec((1, 1, N_MOD * D), lambda i: (mod_row(i), 0, 0)),
            v512, v512, v512,
            pl.BlockSpec((D_RWKV, D_RWKV), const),
            pl.BlockSpec((D, D), const),
            pl.BlockSpec((1, D), const),
            pl.BlockSpec((N_EXPERTS, D), const),
            pl.BlockSpec((N_EXPERTS, D), const),
        ],
        out_specs=[
            pl.BlockSpec((TM, D), tok),
            pl.BlockSpec((TM, D), tok),
            pl.BlockSpec((N_EXPERTS, TM), lambda i: (0, i)),
        ],
        out_shape=[jax.ShapeDtypeStruct((n, D), F32),
                   jax.ShapeDtypeStruct((n, D), BF16),
                   jax.ShapeDtypeStruct((N_EXPERTS, n), F32)],
        compiler_params=_cparams("arbitrary"),
        name="post_mix",
    )(x, yf, yb, r, v, g, kx, yp, mod3, lnw, lnb, rk, bd, wout, g2, wrh, wrl)


MOE_TF = 256


def _moe_kernel(xe_ref, gate_ref, w1_ref, w3_ref, w2_ref, o_ref):
    j = pl.program_id(1)
    xe = xe_ref[0]
    h1 = _dot(xe, w1_ref[0].astype(BF16))
    h3 = _dot(xe, w3_ref[0].astype(BF16))
    he = (h1 * _sigmoid(h1) * h3).astype(BF16)
    part = _dot(he, w2_ref[0].astype(BF16))

    @pl.when(j == 0)
    def _():
        o_ref[0] = part

    @pl.when(j > 0)
    def _():
        o_ref[0] = o_ref[0] + part

    @pl.when(j == pl.num_programs(1) - 1)
    def _():
        o_ref[0] = o_ref[0] * gate_ref[0]


def _moe_call(xe, gate, w1, w3, w2):
    e, c, _ = xe.shape
    return pl.pallas_call(
        _moe_kernel,
        grid=(e, D_EXPERT // MOE_TF),
        in_specs=[
            pl.BlockSpec((1, c, D), lambda ei, j: (ei, 0, 0)),
            pl.BlockSpec((1, c, 1), lambda ei, j: (ei, 0, 0)),
            pl.BlockSpec((1, D, MOE_TF), lambda ei, j: (ei, 0, j)),
            pl.BlockSpec((1, D, MOE_TF), lambda ei, j: (ei, 0, j)),
            pl.BlockSpec((1, MOE_TF, D), lambda ei, j: (ei, j, 0)),
        ],
        out_specs=pl.BlockSpec((1, c, D), lambda ei, j: (ei, 0, 0)),
        out_shape=jax.ShapeDtypeStruct((e, c, D), F32),
        compiler_params=_cparams("arbitrary", "arbitrary"),
        name="moe_ffn",
    )(xe, gate, w1, w3, w2)


def _final_kernel(x_ref, g_ref, o_ref):
    x = x_ref[...]
    o_ref[...] = x * lax.rsqrt(jnp.mean(x * x, axis=-1, keepdims=True) + RMS_EPS) * g_ref[...]


def _final_call(x, gain):
    n = x.shape[0]
    tm = 512
    return pl.pallas_call(
        _final_kernel,
        grid=(n // tm,),
        in_specs=[pl.BlockSpec((tm, D), lambda i: (i, 0)), pl.BlockSpec((1, D), lambda i: (0, 0))],
        out_specs=pl.BlockSpec((tm, D), lambda i: (i, 0)),
        out_shape=jax.ShapeDtypeStruct((n, D), F32),
        compiler_params=_cparams("arbitrary"),
        name="final_norm",
    )(x, gain)


def _heads(u, b, t):
    return u.reshape(b, t, N_HEADS, HEAD)


def _scan_ctx(r, v, kkn, w, kx, bb, b, t):
    def klay(u):
        return _heads(u, b, t).transpose(1, 3, 0, 2).reshape(t, HEAD, b * N_HEADS)

    def both(u):
        f = klay(u)
        return jnp.stack([f, f[::-1]])

    def per_dir(u2):
        return jnp.stack([klay(u2[0]), klay(u2[1])[::-1]])

    s0 = jnp.zeros((2, HEAD, HEAD, LANES), F32)
    y, sfin = _scan_call(both(kkn), per_dir(w), per_dir(bb), per_dir(kx), both(r), both(v), s0)

    def unlay(u):
        return u.reshape(t, HEAD, b, N_HEADS).transpose(2, 0, 3, 1).reshape(b * t, D_RWKV)

    yf = unlay(y[0])
    yb = unlay(y[1][::-1])
    s = sfin.reshape(2, HEAD, HEAD, b, N_HEADS).transpose(3, 0, 4, 2, 1)
    return yf, yb, s


def _scan_smp(r, v, kkn, w, kx, bb, s0_l, b, t):
    nc = b * N_HEADS

    def klay(u):
        return _heads(u, b, t).transpose(1, 3, 0, 2).reshape(t, HEAD, nc)

    def both(u):
        f = klay(u)
        d = jnp.concatenate([f, f[::-1]], axis=-1)
        return jnp.concatenate([d, d], axis=-1)[None]

    def per_dir(u2):
        d = jnp.concatenate([klay(u2[0]), klay(u2[1])[::-1]], axis=-1)
        return jnp.concatenate([d, d], axis=-1)[None]

    hv = HEAD // 2
    vt = _heads(v, b, t).reshape(b, t, N_HEADS, 2, hv).transpose(1, 4, 3, 0, 2)
    vl = jnp.stack([vt, vt[::-1]], axis=3).reshape(1, t, hv, LANES)
    s0 = (s0_l.reshape(b, 2, N_HEADS, 2, hv, HEAD).transpose(5, 4, 3, 1, 0, 2)
          .reshape(1, HEAD, hv, LANES))
    y, _ = _scan_call(both(kkn), per_dir(w), per_dir(bb), per_dir(kx), both(r), vl, s0)
    y = y.reshape(t, hv, 2, 2, b, N_HEADS)

    def unlay(u):
        return u.transpose(3, 0, 4, 2, 1).reshape(b * t, D_RWKV)

    return unlay(y[:, :, :, 0]), unlay(y[:, :, :, 1][::-1])


def kernel(x_prompt, x_sample, state_rwkv, c, c_ctx, norm1, norm2, norm_f, ada_w, ada_b, w_in, mu_shift, w0, w_up, a0, a_up, g_up, k_k, k_a, r_k, ln_x_w, ln_x_b, pool_w, pool_scale, w_out, w_router, w1, w3, w2):
    bc, tc, _ = x_prompt.shape
    bs, ts, _ = x_sample.shape
    nc = bc * tc
    ns = bs * ts
    cap_c = 2 * nc // N_EXPERTS
    cap_s = 2 * ns // N_EXPERTS

    cv = jnp.concatenate([c_ctx[None, :], c, jnp.zeros((8 - 1 - bs, D), F32)], axis=0)
    mod_all = _mod_call(cv, ada_w, ada_b)

    win_b = w_in.astype(BF16)
    wout_b = w_out.astype(BF16)
    gup_b = g_up.astype(BF16)
    pw_b = pool_w.astype(BF16)
    zeros_up = jnp.zeros((DEPTH, HEAD, D_RWKV), F32)

    def blockdiag(u):
        top = jnp.concatenate([u[:, 0], zeros_up], axis=-1)
        bot = jnp.concatenate([zeros_up, u[:, 1]], axis=-1)
        return jnp.concatenate([top, bot], axis=1).astype(BF16)

    wup_b = blockdiag(w_up)
    aup_b = blockdiag(a_up)
    w0_f = w0.reshape(DEPTH, 1, 2 * D_RWKV)
    a0_f = a0.reshape(DEPTH, 1, 2 * D_RWKV)
    wr_t = jnp.swapaxes(w_router, 1, 2)
    wr_hi = wr_t.astype(BF16)
    wr_lo = (wr_t - wr_hi.astype(F32)).astype(BF16)
    bd = jnp.asarray(np.kron(np.eye(N_HEADS, dtype=np.float32), np.ones((HEAD, HEAD), np.float32)), BF16)

    amat_c, inv_c = _pool_constants(tc, False)
    amat_s, inv_s = _pool_constants(ts, True)

    row_c = lambda i: 0
    tiles_s = ts // TM
    row_s = lambda i: 1 + i // tiles_s

    xc = x_prompt.reshape(nc, D)
    xs = x_sample.reshape(ns, D)
    ctx_states = []
    for l in range(DEPTH):
        mod3 = mod_all[l].reshape(8, 1, N_MOD * D)
        row1 = lambda u: u[l].reshape(1, -1)
        lay_in = (row1(norm1), win_b[l], row1(mu_shift), w0_f[l], wup_b[l], a0_f[l], aup_b[l], gup_b[l],
                  row1(k_k), row1(k_a), bd)
        lay_post = (row1(ln_x_w), row1(ln_x_b), row1(r_k), bd, wout_b[l], row1(norm2), wr_hi[l], wr_lo[l])

        r, v, kkn, g, w, kx, bb, zp = _in_call(xc, mod3, row_c, 1, *lay_in)
        yf, yb, sfin = _scan_ctx(r, v, kkn, w, kx, bb, bc, tc)
        ctx_states.append(sfin)
        yp = _pool_seq_call(zp, tc, amat_c, inv_c, pw_b[l], row1(pool_scale))
        x1c, h2c, ptc = _post_call(xc, yf, yb, r, v, g, kx, yp, mod3, row_c, *lay_post)

        r, v, kkn, g, w, kx, bb, zp = _in_call(xs, mod3, row_s, tiles_s, *lay_in)
        yf, yb = _scan_smp(r, v, kkn, w, kx, bb, state_rwkv[:, l], bs, ts)
        yp = _pool_grid_call(zp, ts, amat_s, inv_s, pw_b[l], row1(pool_scale))
        x1s, h2s, pts = _post_call(xs, yf, yb, r, v, g, kx, yp, mod3, row_s, *lay_post)

        gate_c, idx_c = lax.top_k(ptc, cap_c)
        gate_s, idx_s = lax.top_k(pts, cap_s)
        xe = jnp.concatenate([h2c[idx_c], h2s[idx_s]], axis=1)
        gate = jnp.concatenate([gate_c, gate_s], axis=1)[..., None]
        ye = _moe_call(xe, gate, w1[l], w3[l], w2[l])
        out_c = jnp.zeros((nc, D), F32).at[idx_c.reshape(-1)].add(ye[:, :cap_c].reshape(-1, D))
        out_s = jnp.zeros((ns, D), F32).at[idx_s.reshape(-1)].add(ye[:, cap_c:].reshape(-1, D))
        gt2_c = mod_all[l, 0, 5 * D:][None, :]
        gt2_s = jnp.repeat(mod_all[l, 1:1 + bs, 5 * D:], ts, axis=0)
        xc = x1c + gt2_c * out_c
        xs = x1s + gt2_s * out_s

    gf = norm_f.reshape(1, D)
    y_prompt = _final_call(xc, gf).reshape(bc, tc, D)
    y_sample = _final_call(xs, gf).reshape(bs, ts, D)
    new_state = jnp.stack(ctx_states, axis=1)
    return (y_prompt, y_sample, new_state)
```

```python
import functools

import jax
import jax.numpy as jnp
import numpy as np
from jax import lax
from jax.experimental import pallas as pl
from jax.experimental.pallas import tpu as pltpu

F32 = jnp.float32
BF16 = jnp.bfloat16

D = 1024
DEPTH = 4
GRID_W = 64
D_RWKV = 512
HEAD = 64
N_HEADS = 8
POOL_WINDOWS = (2, 4, 8, 16)
POOL_GROUP = 128
N_EXPERTS = 16
D_EXPERT = 2048
N_MOD = 6
RMS_EPS = 1e-6
GN_EPS = 64e-5
OFF_K = 512
OFF_V = 1024
OFF_G = 1536
OFF_W = 1664
OFF_A = 1792
OFF_POOL = 1920
D_IN = 2432

VMEM_LIMIT = 52 * 1024 * 1024
TM = 256
TB = 32
LANES = 128


def _cparams(*sem):
    return pltpu.CompilerParams(dimension_semantics=sem, vmem_limit_bytes=VMEM_LIMIT)


def _sigmoid(x):
    return 1.0 / (1.0 + jnp.exp(-x))


def _dot(a, b):
    return jnp.dot(a, b, preferred_element_type=F32)


def _split_dot(x, m):
    hi = x.astype(BF16)
    lo = (x - hi.astype(F32)).astype(BF16)
    return _dot(hi, m) + _dot(lo, m)


def _split_dot_left(m, x):
    hi = x.astype(BF16)
    lo = (x - hi.astype(F32)).astype(BF16)
    return _dot(m, hi) + _dot(m, lo)


N_PAIRS = N_HEADS // 2


def _put_pairs(o_ref, lead, val):
    for hp in range(N_PAIRS):
        o_ref[lead + (slice(None), 0, hp, slice(None))] = val[:, hp * LANES:(hp + 1) * LANES]


def _get_pairs(ref, lead):
    return jnp.concatenate(
        [ref[lead + (slice(None), 0, hp, slice(None))] for hp in range(N_PAIRS)], axis=1)


def _mod_kernel(cv_ref, w_ref, b_ref, o_ref):
    cv = cv_ref[...]
    s = (cv * _sigmoid(cv)).astype(BF16)
    o_ref[0] = _dot(s, w_ref[0].astype(BF16)) + b_ref[0]


def _mod_call(cv, ada_w, ada_b):
    tn = 1536
    n_out = N_MOD * D
    return pl.pallas_call(
        _mod_kernel,
        grid=(DEPTH, n_out // tn),
        in_specs=[
            pl.BlockSpec((8, D), lambda l, j: (0, 0)),
            pl.BlockSpec((1, D, tn), lambda l, j: (l, 0, j)),
            pl.BlockSpec((1, 1, tn), lambda l, j: (l, 0, j)),
        ],
        out_specs=pl.BlockSpec((1, 8, tn), lambda l, j: (l, 0, j)),
        out_shape=jax.ShapeDtypeStruct((DEPTH, 8, n_out), F32),
        compiler_params=_cparams("arbitrary", "arbitrary"),
        name="adaln_mod",
    )(cv, ada_w, ada_b.reshape(DEPTH, 1, n_out))


def _in_kernel(x_ref, xp_ref, xn_ref, mod_ref, g1_ref, win_ref, mu_ref, w0_ref, wup_ref, a0_ref,
               aup_ref, gup_ref, kk_ref, ka_ref, bd_ref,
               r_o, v_o, kkn_o, g_o, w_o, kx_o, b_o, zp_o, *, tiles_per_seq):
    mod = mod_ref[0]
    sh1 = mod[:, 0:D]
    sc1 = mod[:, D:2 * D]
    g1 = g1_ref[...]

    def hfun(x):
        y = x * lax.rsqrt(jnp.mean(x * x, axis=-1, keepdims=True) + RMS_EPS)
        return ((y * g1) * (1.0 + sc1) + sh1).astype(BF16)

    p = _dot(hfun(x_ref[...]), win_ref[...])
    z = p[:, :OFF_POOL]
    rows = lax.broadcasted_iota(jnp.int32, z.shape, 0)
    prev = pltpu.roll(z, 1, 0)
    nxt = pltpu.roll(z, TM - 1, 0)
    if tiles_per_seq > 1:
        j = pl.program_id(0) % tiles_per_seq
        wz = win_ref[:, :OFF_POOL]
        pp = _dot(hfun(xp_ref[...]), wz)
        pn = _dot(hfun(xn_ref[...]), wz)
        prow = jnp.where(j > 0, pp[7:8, :], 0.0)
        nrow = jnp.where(j < tiles_per_seq - 1, pn[0:1, :], 0.0)
    else:
        prow = 0.0
        nrow = 0.0
    prev = jnp.where(rows == 0, prow, prev)
    nxt = jnp.where(rows == TM - 1, nrow, nxt)
    zs = z + mu_ref[...] * (0.5 * (prev + nxt) - z)

    r = zs[:, :OFF_K]
    k = zs[:, OFF_K:OFF_V]
    v = zs[:, OFF_V:OFF_G]
    gd = zs[:, OFF_G:OFF_W]
    wd = zs[:, OFF_W:OFF_A]
    ad = zs[:, OFF_A:OFF_POOL]

    wl = w0_ref[...] + _dot(jnp.tanh(wd).astype(BF16), wup_ref[...])
    decay = jnp.exp(-np.float32(np.exp(-0.5)) * _sigmoid(wl))
    a = _sigmoid(a0_ref[...] + _dot(ad.astype(BF16), aup_ref[...]))
    g = _dot(_sigmoid(gd).astype(BF16), gup_ref[...])

    kkr = k * kk_ref[...]
    ssq = _split_dot(kkr * kkr, bd_ref[...])
    kkn = kkr / jnp.maximum(jnp.sqrt(ssq), 1e-12)

    _put_pairs(r_o, (), r)
    _put_pairs(v_o, (), v)
    _put_pairs(kkn_o, (), kkn)
    g_o[...] = g
    zp_o[...] = p[:, OFF_POOL:]
    ka = ka_ref[...]
    for d in range(2):
        a_d = a[:, d * D_RWKV:(d + 1) * D_RWKV]
        _put_pairs(w_o, (d,), decay[:, d * D_RWKV:(d + 1) * D_RWKV])
        _put_pairs(kx_o, (d,), k * (1.0 + (a_d - 1.0) * ka))
        _put_pairs(b_o, (d,), kkn * a_d)


def _pair_spec(tiles_per_seq, lead):
    zeros = (0,) * lead
    return pl.BlockSpec(
        (2,) * lead + (TM, 1, N_PAIRS, LANES),
        lambda i: zeros + (i % tiles_per_seq, i // tiles_per_seq, 0, 0))


def _in_call(x, mod3, mod_row, tiles_per_seq, g1, win, mu, w0, wup, a0, aup, gup, kk_k, k_a, bd):
    n = x.shape[0]
    nt = n // TM
    nb8 = n // 8
    t = tiles_per_seq * TM
    b = n // t
    const = lambda i: (0, 0)
    tok = lambda i: (i, 0)
    f512 = jax.ShapeDtypeStruct((n, D_RWKV), F32)
    p1 = jax.ShapeDtypeStruct((t, b, N_PAIRS, LANES), F32)
    p2 = jax.ShapeDtypeStruct((2, t, b, N_PAIRS, LANES), F32)
    ps1 = _pair_spec(tiles_per_seq, 0)
    ps2 = _pair_spec(tiles_per_seq, 1)
    return pl.pallas_call(
        functools.partial(_in_kernel, tiles_per_seq=tiles_per_seq),
        grid=(nt,),
        in_specs=[
            pl.BlockSpec((TM, D), tok),
            pl.BlockSpec((8, D), lambda i: (jnp.maximum(i * (TM // 8) - 1, 0), 0)),
            pl.BlockSpec((8, D), lambda i: (jnp.minimum((i + 1) * (TM // 8), nb8 - 1), 0)),
            pl.BlockSpec((1, 1, N_MOD * D), lambda i: (mod_row(i), 0, 0)),
            pl.BlockSpec((1, D), const),
            pl.BlockSpec((D, D_IN), const),
            pl.BlockSpec((1, OFF_POOL), const),
            pl.BlockSpec((1, 2 * D_RWKV), const),
            pl.BlockSpec((128, 2 * D_RWKV), const),
            pl.BlockSpec((1, 2 * D_RWKV), const),
            pl.BlockSpec((128, 2 * D_RWKV), const),
            pl.BlockSpec((128, D_RWKV), const),
            pl.BlockSpec((1, D_RWKV), const),
            pl.BlockSpec((1, D_RWKV), const),
            pl.BlockSpec((D_RWKV, D_RWKV), const),
        ],
        out_specs=[ps1, ps1, ps1, pl.BlockSpec((TM, D_RWKV), tok), ps2, ps2, ps2,
                   pl.BlockSpec((TM, D_RWKV), tok)],
        out_shape=[p1, p1, p1, f512, p2, p2, p2, f512],
        compiler_params=_cparams("arbitrary"),
        name="in_proj",
    )(x, x, x, mod3, g1, win, mu, w0, wup, a0, aup, gup, kk_k, k_a, bd)


def _scan_fill(dst, srcs, i, ng):
    ib = TB - 1 - i
    for a, (f_ref, b_ref, lead) in enumerate(srcs):
        base = jnp.concatenate([f_ref[lead + (i,)], b_ref[lead + (ib,)]], axis=0)
        if ng == 2:
            tile = base
        elif a == len(srcs) - 1:
            tile = jnp.concatenate([base, pltpu.roll(base, 64, 1), pltpu.roll(base, 96, 1),
                                    pltpu.roll(base, 32, 1)], axis=0)
        else:
            swapped = pltpu.roll(base, 64, 1)
            tile = jnp.concatenate([base, swapped, base, swapped], axis=0)
        dst[a][...] = tile.T[0:dst[a].shape[0], :]


def _scan_step(src, s_ref, yf_ref, yb_ref, i, ng, vp):
    kk_s, w_s, b_s, kx_s, r_s, v_s = src
    ib = TB - 1 - i
    ys = []
    for g in range(ng):
        o = g * HEAD
        sa0 = jnp.zeros((vp, LANES), F32)
        sa1 = jnp.zeros((vp, LANES), F32)
        for k in range(0, HEAD, 2):
            sa0 = sa0 + s_ref[g, k] * kk_s[pl.ds(o + k, 1), :]
            sa1 = sa1 + s_ref[g, k + 1] * kk_s[pl.ds(o + k + 1, 1), :]
        sa = sa0 + sa1
        vv = v_s[g * vp:(g + 1) * vp, :]
        y0 = jnp.zeros((vp, LANES), F32)
        y1 = jnp.zeros((vp, LANES), F32)
        for k in range(HEAD):
            sn = (s_ref[g, k] * w_s[pl.ds(o + k, 1), :] - sa * b_s[pl.ds(o + k, 1), :]
                  + vv * kx_s[pl.ds(o + k, 1), :])
            s_ref[g, k] = sn
            if k % 2 == 0:
                y0 = y0 + sn * r_s[pl.ds(o + k, 1), :]
            else:
                y1 = y1 + sn * r_s[pl.ds(o + k, 1), :]
        ys.append(y0 + y1)
    if ng == 2:
        yt = jnp.concatenate(ys, axis=0).T
    else:
        sq = jnp.concatenate([ys[0], jnp.zeros((LANES - vp, LANES), F32)], axis=0).T
        q = LANES // 4
        yt = (sq[0:q] + pltpu.roll(sq[q:2 * q], 64, 1) + pltpu.roll(sq[2 * q:3 * q], 32, 1)
              + pltpu.roll(sq[3 * q:4 * q], 96, 1))
    half = yt.shape[0] // 2
    yf_ref[i] = yt[0:half]
    yb_ref[ib] = yt[half:]


def _scan_kernel(kkf, kkb, rf, rb, vf, vb, wf, wb, bf, bb, kxf, kxb, s0_ref, yf_ref, yb_ref, sfin_ref,
                 s_ref, *scr, ng, vp):
    tb = pl.program_id(0)

    @pl.when(tb == 0)
    def _():
        s_ref[...] = s0_ref[...]

    srcs = [(kkf, kkb, ()), (wf, wb, (0,)), (bf, bb, (0,)), (kxf, kxb, (0,)), (rf, rb, ()), (vf, vb, ())]
    set_a, set_b = scr[:6], scr[6:]
    _scan_fill(set_a, srcs, 0, ng)

    def pair(j, carry):
        i0 = 2 * j
        _scan_fill(set_b, srcs, i0 + 1, ng)
        _scan_step(set_a, s_ref, yf_ref, yb_ref, i0, ng, vp)
        _scan_fill(set_a, srcs, jnp.minimum(i0 + 2, TB - 1), ng)
        _scan_step(set_b, s_ref, yf_ref, yb_ref, i0 + 1, ng, vp)
        return carry

    lax.fori_loop(0, TB // 2, pair, 0)

    @pl.when(tb == pl.num_programs(0) - 1)
    def _():
        sfin_ref[...] = s_ref[...]


def _scan_call(kk, r, v, w, bb, kx, s0):
    t, rows, _ = kk.shape
    ng, _, vp, _ = s0.shape
    nt = t // TB
    f1 = pl.BlockSpec((TB, rows, LANES), lambda ti: (ti, 0, 0))
    b1 = pl.BlockSpec((TB, rows, LANES), lambda ti: (nt - 1 - ti, 0, 0))
    f2 = pl.BlockSpec((1, TB, rows, LANES), lambda ti: (0, ti, 0, 0))
    b2 = pl.BlockSpec((1, TB, rows, LANES), lambda ti: (1, nt - 1 - ti, 0, 0))
    sspec = pl.BlockSpec((ng, HEAD, vp, LANES), lambda ti: (0, 0, 0, 0))
    krows = ng * HEAD
    scratch = [pltpu.VMEM((ng, HEAD, vp, LANES), F32)]
    for _ in range(2):
        scratch += [pltpu.VMEM((krows, LANES), F32)] * 5 + [pltpu.VMEM((ng * vp, LANES), F32)]
    ysd = jax.ShapeDtypeStruct((t, rows, LANES), F32)
    return pl.pallas_call(
        functools.partial(_scan_kernel, ng=ng, vp=vp),
        grid=(nt,),
        in_specs=[f1, b1, f1, b1, f1, b1, f2, b2, f2, b2, f2, b2, sspec],
        out_specs=[f1, b1, sspec],
        out_shape=[ysd, ysd, jax.ShapeDtypeStruct((ng, HEAD, vp, LANES), F32)],
        scratch_shapes=scratch,
        compiler_params=_cparams("arbitrary"),
        name="wkv_scan",
    )(kk, kk, r, r, v, v, w, w, bb, bb, kx, kx, s0)


def _pool_seq_kernel(z_ref, a_ref, inv_ref, pw_ref, ps_ref, o_ref):
    for gi in range(len(POOL_WINDOWS)):
        sl = slice(gi * POOL_GROUP, (gi + 1) * POOL_GROUP)
        zg = z_ref[:, sl]
        s = _split_dot_left(a_ref[gi], zg)
        d = s * inv_ref[:, sl] - zg
        o_ref[:, sl] = _dot(d.astype(BF16), pw_ref[gi]) * ps_ref[:, sl]


def _pool_seq_call(z, t, amat, inv, pw, ps):
    n = z.shape[0]
    ng = len(POOL_WINDOWS)
    return pl.pallas_call(
        _pool_seq_kernel,
        grid=(n // t,),
        in_specs=[
            pl.BlockSpec((t, D_RWKV), lambda i: (i, 0)),
            pl.BlockSpec((ng, t, t), lambda i: (0, 0, 0)),
            pl.BlockSpec((t, D_RWKV), lambda i: (0, 0)),
            pl.BlockSpec((ng, POOL_GROUP, POOL_GROUP), lambda i: (0, 0, 0)),
            pl.BlockSpec((1, D_RWKV), lambda i: (0, 0)),
        ],
        out_specs=pl.BlockSpec((t, D_RWKV), lambda i: (i, 0)),
        out_shape=jax.ShapeDtypeStruct((n, D_RWKV), F32),
        compiler_params=_cparams("arbitrary"),
        name="pool_seq",
    )(z, amat, inv, pw, ps)


POOL_BLK = 256
POOL_PAD = (max(POOL_WINDOWS) // 2) * GRID_W


def _pool_grid_kernel(z_ref, a_ref, inv_ref, pw_ref, ps_ref, o_ref, pad_ref, *, t):
    zeros = jnp.zeros((POOL_PAD, POOL_GROUP), F32)
    pad_ref[0:POOL_PAD, :] = zeros
    pad_ref[POOL_PAD + t:POOL_PAD + t + POOL_PAD, :] = zeros
    for gi, win in enumerate(POOL_WINDOWS):
        sl = slice(gi * POOL_GROUP, (gi + 1) * POOL_GROUP)
        for blk in range(t // POOL_BLK):
            zb = z_ref[blk * POOL_BLK:(blk + 1) * POOL_BLK, sl]
            pad_ref[POOL_PAD + blk * POOL_BLK:POOL_PAD + (blk + 1) * POOL_BLK, :] = (
                _split_dot_left(a_ref[gi], zb))
        lo = win // 2
        hi = win - 1 - lo
        s = pad_ref[POOL_PAD - lo * GRID_W:POOL_PAD - lo * GRID_W + t, :]
        for j in range(-lo + 1, hi + 1):
            s = s + pad_ref[POOL_PAD + j * GRID_W:POOL_PAD + j * GRID_W + t, :]
        d = s * inv_ref[:, sl] - z_ref[:, sl]
        o_ref[:, sl] = _dot(d.astype(BF16), pw_ref[gi]) * ps_ref[:, sl]


def _pool_grid_call(z, t, amat, inv, pw, ps):
    n = z.shape[0]
    ng = len(POOL_WINDOWS)
    return pl.pallas_call(
        functools.partial(_pool_grid_kernel, t=t),
        grid=(n // t,),
        in_specs=[
            pl.BlockSpec((t, D_RWKV), lambda i: (i, 0)),
            pl.BlockSpec((ng, POOL_BLK, POOL_BLK), lambda i: (0, 0, 0)),
            pl.BlockSpec((t, D_RWKV), lambda i: (0, 0)),
            pl.BlockSpec((ng, POOL_GROUP, POOL_GROUP), lambda i: (0, 0, 0)),
            pl.BlockSpec((1, D_RWKV), lambda i: (0, 0)),
        ],
        out_specs=pl.BlockSpec((t, D_RWKV), lambda i: (i, 0)),
        out_shape=jax.ShapeDtypeStruct((n, D_RWKV), F32),
        scratch_shapes=[pltpu.VMEM((t + 2 * POOL_PAD, POOL_GROUP), F32)],
        compiler_params=_cparams("arbitrary"),
        name="pool_grid",
    )(z, amat, inv, pw, ps)


def _band(n, win):
    lo = win // 2
    hi = win - 1 - lo
    pos = np.arange(n)
    start = np.clip(pos - lo, 0, n)
    end = np.clip(pos + hi + 1, 0, n)
    m = (pos[None, :] >= start[:, None]) & (pos[None, :] < end[:, None])
    return m.astype(np.float32), (end - start).astype(np.float32)


def _pool_constants(t, use_grid):
    mats, invs = [], []
    for win in POOL_WINDOWS:
        if use_grid:
            rows = t // GRID_W
            mcol, ccol = _band(GRID_W, win)
            _, crow = _band(rows, win)
            mats.append(np.kron(np.eye(POOL_BLK // GRID_W, dtype=np.float32), mcol))
            cnt = (crow[:, None] * ccol[None, :]).reshape(t)
        else:
            m, cnt = _band(t, win)
            mats.append(m)
        invs.append(np.repeat((1.0 / cnt)[:, None], POOL_GROUP, axis=1))
    return (jnp.asarray(np.stack(mats), BF16),
            jnp.asarray(np.concatenate(invs, axis=1), F32))


def _post_kernel(x_ref, yf_ref, yb_ref, r_ref, v_ref, g_ref, kx_ref, yp_ref, mod_ref, lnw_ref,
                 lnb_ref, rk_ref, bd_ref, wout_ref, g2_ref, wrh_ref, wrl_ref,
                 x1_o, h2_o, pt_o):
    mod = mod_ref[0]
    gt1 = mod[:, 2 * D:3 * D]
    sh2 = mod[:, 3 * D:4 * D]
    sc2 = mod[:, 4 * D:5 * D]
    bd = bd_ref[...]
    inv_n = np.float32(1.0 / HEAD)

    y = _get_pairs(yf_ref, ()) + _get_pairs(yb_ref, ())
    mu = _split_dot(y, bd) * inv_n
    yc = y - mu
    var = _split_dot(yc * yc, bd) * inv_n
    yn = yc * lax.rsqrt(var + GN_EPS) * lnw_ref[...] + lnb_ref[...]
    v = _get_pairs(v_ref, ())
    kxs = _get_pairs(kx_ref, (0,)) + _get_pairs(kx_ref, (1,))
    bonus = _split_dot(_get_pairs(r_ref, ()) * kxs * rk_ref[...], bd) * v
    yr = ((yn + bonus) * g_ref[...]).astype(BF16)
    m = _dot(yr, wout_ref[0:D_RWKV, :]) + _dot(yp_ref[...].astype(BF16), wout_ref[D_RWKV:D, :])
    x1 = x_ref[...] + gt1 * m
    x1_o[...] = x1
    h = x1 * lax.rsqrt(jnp.mean(x1 * x1, axis=-1, keepdims=True) + RMS_EPS)
    h2 = (h * g2_ref[...]) * (1.0 + sc2) + sh2
    h2_o[...] = h2.astype(BF16)

    hi = h2.astype(BF16)
    lo = (h2 - hi.astype(F32)).astype(BF16)
    dn = (((1,), (1,)), ((), ()))
    logits = (lax.dot_general(wrh_ref[...], hi, dn, preferred_element_type=F32)
              + lax.dot_general(wrh_ref[...], lo, dn, preferred_element_type=F32)
              + lax.dot_general(wrl_ref[...], hi, dn, preferred_element_type=F32))
    mx = jnp.max(logits, axis=0, keepdims=True)
    e = jnp.exp(logits - mx)
    pt_o[...] = e / jnp.sum(e, axis=0, keepdims=True)


def _post_call(x, yf, yb, r, v, g, kx, yp, mod3, mod_row, tiles_per_seq, lnw, lnb, rk, bd, wout, g2,
               wrh, wrl):
    n = x.shape[0]
    const = lambda i: (0, 0)
    tok = lambda i: (i, 0)
    t512 = pl.BlockSpec((TM, D_RWKV), tok)
    v512 = pl.BlockSpec((1, D_RWKV), const)
    ps1 = _pair_spec(tiles_per_seq, 0)
    return pl.pallas_call(
        _post_kernel,
        grid=(n // TM,),
        in_specs=[
            pl.BlockSpec((TM, D), tok), ps1, ps1, ps1, ps1, t512,
            _pair_spec(tiles_per_seq, 1),
            t512,
            pl.BlockSpec((1, 1, N_MOD * D), lambda i: (mod_row(i), 0, 0)),
            v512, v512, v512,
            pl.BlockSpec((D_RWKV, D_RWKV), const),
            pl.BlockSpec((D, D), const),
            pl.BlockSpec((1, D), const),
            pl.BlockSpec((N_EXPERTS, D), const),
            pl.BlockSpec((N_EXPERTS, D), const),
        ],
        out_specs=[
            pl.BlockSpec((TM, D), tok),
            pl.BlockSpec((TM, D), tok),
            pl.BlockSpec((N_EXPERTS, TM), lambda i: (0, i)),
        ],
        out_shape=[jax.ShapeDtypeStruct((n, D), F32),
                   jax.ShapeDtypeStruct((n, D), BF16),
                   jax.ShapeDtypeStruct((N_EXPERTS, n), F32)],
        compiler_params=_cparams("arbitrary"),
        name="post_mix",
    )(x, yf, yb, r, v, g, kx, yp, mod3, lnw, lnb, rk, bd, wout, g2, wrh, wrl)


MOE_TF = 256


def _moe_kernel(xe_ref, gate_ref, w1_ref, w3_ref, w2_ref, o_ref):
    j = pl.program_id(1)
    xe = xe_ref[0]
    h1 = _dot(xe, w1_ref[0, 0].astype(BF16))
    h3 = _dot(xe, w3_ref[0, 0].astype(BF16))
    he = (h1 * _sigmoid(h1) * h3).astype(BF16)
    part = _dot(he, w2_ref[0, 0].astype(BF16))

    @pl.when(j == 0)
    def _():
        o_ref[0] = part

    @pl.when(j > 0)
    def _():
        o_ref[0] = o_ref[0] + part

    @pl.when(j == pl.num_programs(1) - 1)
    def _():
        o_ref[0] = o_ref[0] * gate_ref[0]


def _moe_call(xe, gate, w1, w3, w2, layer):
    e, c, _ = xe.shape
    return pl.pallas_call(
        _moe_kernel,
        grid=(e, D_EXPERT // MOE_TF),
        in_specs=[
            pl.BlockSpec((1, c, D), lambda ei, j: (ei, 0, 0)),
            pl.BlockSpec((1, c, 1), lambda ei, j: (ei, 0, 0)),
            pl.BlockSpec((1, 1, D, MOE_TF), lambda ei, j: (layer, ei, 0, j)),
            pl.BlockSpec((1, 1, D, MOE_TF), lambda ei, j: (layer, ei, 0, j)),
            pl.BlockSpec((1, 1, MOE_TF, D), lambda ei, j: (layer, ei, j, 0)),
        ],
        out_specs=pl.BlockSpec((1, c, D), lambda ei, j: (ei, 0, 0)),
        out_shape=jax.ShapeDtypeStruct((e, c, D), F32),
        compiler_params=_cparams("arbitrary", "arbitrary"),
        name="moe_ffn",
    )(xe, gate, w1, w3, w2)


def _final_kernel(x_ref, g_ref, o_ref):
    x = x_ref[...]
    o_ref[...] = x * lax.rsqrt(jnp.mean(x * x, axis=-1, keepdims=True) + RMS_EPS) * g_ref[...]


def _final_call(x, gain):
    n = x.shape[0]
    tm = 512
    return pl.pallas_call(
        _final_kernel,
        grid=(n // tm,),
        in_specs=[pl.BlockSpec((tm, D), lambda i: (i, 0)), pl.BlockSpec((1, D), lambda i: (0, 0))],
        out_specs=pl.BlockSpec((tm, D), lambda i: (i, 0)),
        out_shape=jax.ShapeDtypeStruct((n, D), F32),
        compiler_params=_cparams("arbitrary"),
        name="final_norm",
    )(x, gain)


def _scan_pass(r, v, kkn, w, kx, bb, s0):
    t, b = r.shape[0], r.shape[1]
    rows = b * N_PAIRS
    flat = lambda u: u.reshape(u.shape[:-3] + (rows, LANES))
    yf, yb, sfin = _scan_call(flat(kkn), flat(r), flat(v), flat(w), flat(bb), flat(kx), s0)
    unflat = lambda u: u.reshape(t, b, N_PAIRS, LANES)
    return unflat(yf), unflat(yb), sfin


def _state_to_lanes(s):
    b = s.shape[0]
    hv = HEAD // 2
    return (s.reshape(b, 2, N_PAIRS, 2, 2, hv, HEAD).transpose(6, 5, 4, 3, 1, 0, 2)
            .reshape(1, HEAD, hv, LANES))


def _state_from_lanes(s, b):
    return (s.reshape(2, HEAD, HEAD, 2, b, N_PAIRS).transpose(4, 3, 5, 0, 2, 1)
            .reshape(b, 2, N_HEADS, HEAD, HEAD))


def kernel(x_prompt, x_sample, state_rwkv, c, c_ctx, norm1, norm2, norm_f, ada_w, ada_b, w_in, mu_shift, w0, w_up, a0, a_up, g_up, k_k, k_a, r_k, ln_x_w, ln_x_b, pool_w, pool_scale, w_out, w_router, w1, w3, w2):
    bc, tc, _ = x_prompt.shape
    bs, ts, _ = x_sample.shape
    nc = bc * tc
    ns = bs * ts
    cap_c = 2 * nc // N_EXPERTS
    cap_s = 2 * ns // N_EXPERTS

    cv = jnp.concatenate([c_ctx[None, :], c, jnp.zeros((8 - 1 - bs, D), F32)], axis=0)
    mod_all = _mod_call(cv, ada_w, ada_b)

    win_b = w_in.astype(BF16)
    wout_b = w_out.astype(BF16)
    gup_b = g_up.astype(BF16)
    pw_b = pool_w.astype(BF16)
    zeros_up = jnp.zeros((DEPTH, HEAD, D_RWKV), F32)

    def blockdiag(u):
        top = jnp.concatenate([u[:, 0], zeros_up], axis=-1)
        bot = jnp.concatenate([zeros_up, u[:, 1]], axis=-1)
        return jnp.concatenate([top, bot], axis=1).astype(BF16)

    wup_b = blockdiag(w_up)
    aup_b = blockdiag(a_up)
    w0_f = w0.reshape(DEPTH, 1, 2 * D_RWKV)
    a0_f = a0.reshape(DEPTH, 1, 2 * D_RWKV)
    wr_t = jnp.swapaxes(w_router, 1, 2)
    wr_hi = wr_t.astype(BF16)
    wr_lo = (wr_t - wr_hi.astype(F32)).astype(BF16)
    bd = jnp.asarray(np.kron(np.eye(N_HEADS, dtype=np.float32), np.ones((HEAD, HEAD), np.float32)), BF16)

    amat_c, inv_c = _pool_constants(tc, False)
    amat_s, inv_s = _pool_constants(ts, True)

    row_c = lambda i: 0
    tiles_s = ts // TM
    row_s = lambda i: 1 + i // tiles_s

    xc = x_prompt.reshape(nc, D)
    xs = x_sample.reshape(ns, D)
    ctx_states = []
    zero_state = jnp.zeros((2, HEAD, HEAD, LANES), F32)
    for l in range(DEPTH):
        mod3 = mod_all[l].reshape(8, 1, N_MOD * D)
        row1 = lambda u: u[l].reshape(1, -1)
        lay_in = (row1(norm1), win_b[l], row1(mu_shift), w0_f[l], wup_b[l], a0_f[l], aup_b[l], gup_b[l],
                  row1(k_k), row1(k_a), bd)
        lay_post = (row1(ln_x_w), row1(ln_x_b), row1(r_k), bd, wout_b[l], row1(norm2), wr_hi[l], wr_lo[l])

        r, v, kkn, g, w, kx, bb, zp = _in_call(xc, mod3, row_c, 1, *lay_in)
        yf, yb, sfin = _scan_pass(r, v, kkn, w, kx, bb, zero_state)
        ctx_states.append(_state_from_lanes(sfin, bc))
        yp = _pool_seq_call(zp, tc, amat_c, inv_c, pw_b[l], row1(pool_scale))
        x1c, h2c, ptc = _post_call(xc, yf, yb, r, v, g, kx, yp, mod3, row_c, 1, *lay_post)

        r, v, kkn, g, w, kx, bb, zp = _in_call(xs, mod3, row_s, tiles_s, *lay_in)
        yf, yb, _ = _scan_pass(r, v, kkn, w, kx, bb, _state_to_lanes(state_rwkv[:, l]))
        yp = _pool_grid_call(zp, ts, amat_s, inv_s, pw_b[l], row1(pool_scale))
        x1s, h2s, pts = _post_call(xs, yf, yb, r, v, g, kx, yp, mod3, row_s, tiles_s, *lay_post)

        gate_c, idx_c = lax.top_k(ptc, cap_c)
        gate_s, idx_s = lax.top_k(pts, cap_s)
        xe = jnp.concatenate([h2c[idx_c], h2s[idx_s]], axis=1)
        gate = jnp.concatenate([gate_c, gate_s], axis=1)[..., None]
        ye = _moe_call(xe, gate, w1, w3, w2, l)
        out_c = jnp.zeros((nc, D), F32).at[idx_c.reshape(-1)].add(ye[:, :cap_c].reshape(-1, D))
        out_s = jnp.zeros((ns, D), F32).at[idx_s.reshape(-1)].add(ye[:, cap_c:].reshape(-1, D))
        gt2_c = mod_all[l, 0, 5 * D:][None, :]
        gt2_s = jnp.repeat(mod_all[l, 1:1 + bs, 5 * D:], ts, axis=0)
        xc = x1c + gt2_c * out_c
        xs = x1s + gt2_s * out_s

    gf = norm_f.reshape(1, D)
    y_prompt = _final_call(xc, gf).reshape(bc, tc, D)
    y_sample = _final_call(xs, gf).reshape(bs, ts, D)
    new_state = jnp.stack(ctx_states, axis=1)
    return (y_prompt, y_sample, new_state)
```

```python
import functools

import jax
import jax.numpy as jnp
import numpy as np
from jax import lax
from jax.experimental import pallas as pl
from jax.experimental.pallas import tpu as pltpu

F32 = jnp.float32
BF16 = jnp.bfloat16

D = 1024
DEPTH = 4
GRID_W = 64
D_RWKV = 512
HEAD = 64
N_HEADS = 8
POOL_WINDOWS = (2, 4, 8, 16)
POOL_GROUP = 128
N_EXPERTS = 16
D_EXPERT = 2048
N_MOD = 6
RMS_EPS = 1e-6
GN_EPS = 64e-5
OFF_K = 512
OFF_V = 1024
OFF_G = 1536
OFF_W = 1664
OFF_A = 1792
OFF_POOL = 1920
D_IN = 2432

VMEM_LIMIT = 52 * 1024 * 1024
TM = 256
TB = 32
LANES = 128


def _cparams(*sem):
    return pltpu.CompilerParams(dimension_semantics=sem, vmem_limit_bytes=VMEM_LIMIT)


def _sigmoid(x):
    return 1.0 / (1.0 + jnp.exp(-x))


def _dot(a, b):
    return jnp.dot(a, b, preferred_element_type=F32)


def _split_dot(x, m):
    hi = x.astype(BF16)
    lo = (x - hi.astype(F32)).astype(BF16)
    return _dot(hi, m) + _dot(lo, m)


def _split_dot_left(m, x):
    hi = x.astype(BF16)
    lo = (x - hi.astype(F32)).astype(BF16)
    return _dot(m, hi) + _dot(m, lo)


N_PAIRS = N_HEADS // 2


def _mod_kernel(cv_ref, w_ref, b_ref, o_ref):
    cv = cv_ref[...]
    s = (cv * _sigmoid(cv)).astype(BF16)
    o_ref[0] = _dot(s, w_ref[0].astype(BF16)) + b_ref[0]


def _mod_call(cv, ada_w, ada_b):
    tn = 1536
    n_out = N_MOD * D
    return pl.pallas_call(
        _mod_kernel,
        grid=(DEPTH, n_out // tn),
        in_specs=[
            pl.BlockSpec((8, D), lambda l, j: (0, 0)),
            pl.BlockSpec((1, D, tn), lambda l, j: (l, 0, j)),
            pl.BlockSpec((1, 1, tn), lambda l, j: (l, 0, j)),
        ],
        out_specs=pl.BlockSpec((1, 8, tn), lambda l, j: (l, 0, j)),
        out_shape=jax.ShapeDtypeStruct((DEPTH, 8, n_out), F32),
        compiler_params=_cparams("arbitrary", "arbitrary"),
        name="adaln_mod",
    )(cv, ada_w, ada_b.reshape(DEPTH, 1, n_out))


def _in_kernel(x_ref, xp_ref, xn_ref, mod_ref, g1_ref, win_ref, mu_ref, w0_ref, wup_ref, a0_ref,
               aup_ref, gup_ref, kk_ref, ka_ref, bd_ref,
               r_o, v_o, kkn_o, g_o, w_o, kx_o, b_o, zp_o, *, tiles_per_seq):
    mod = mod_ref[0]
    sh1 = mod[:, 0:D]
    sc1 = mod[:, D:2 * D]
    g1 = g1_ref[...]

    def hfun(x):
        y = x * lax.rsqrt(jnp.mean(x * x, axis=-1, keepdims=True) + RMS_EPS)
        return ((y * g1) * (1.0 + sc1) + sh1).astype(BF16)

    p = _dot(hfun(x_ref[...]), win_ref[...])
    z = p[:, :OFF_POOL]
    rows = lax.broadcasted_iota(jnp.int32, z.shape, 0)
    prev = pltpu.roll(z, 1, 0)
    nxt = pltpu.roll(z, TM - 1, 0)
    if tiles_per_seq > 1:
        j = pl.program_id(0) % tiles_per_seq
        wz = win_ref[:, :OFF_POOL]
        pp = _dot(hfun(xp_ref[...]), wz)
        pn = _dot(hfun(xn_ref[...]), wz)
        prow = jnp.where(j > 0, pp[7:8, :], 0.0)
        nrow = jnp.where(j < tiles_per_seq - 1, pn[0:1, :], 0.0)
    else:
        prow = 0.0
        nrow = 0.0
    prev = jnp.where(rows == 0, prow, prev)
    nxt = jnp.where(rows == TM - 1, nrow, nxt)
    zs = z + mu_ref[...] * (0.5 * (prev + nxt) - z)

    r = zs[:, :OFF_K]
    k = zs[:, OFF_K:OFF_V]
    v = zs[:, OFF_V:OFF_G]
    gd = zs[:, OFF_G:OFF_W]
    wd = zs[:, OFF_W:OFF_A]
    ad = zs[:, OFF_A:OFF_POOL]

    wl = w0_ref[...] + _dot(jnp.tanh(wd).astype(BF16), wup_ref[...])
    decay = jnp.exp(-np.float32(np.exp(-0.5)) * _sigmoid(wl))
    a = _sigmoid(a0_ref[...] + _dot(ad.astype(BF16), aup_ref[...]))
    g = _dot(_sigmoid(gd).astype(BF16), gup_ref[...])

    kkr = k * kk_ref[...]
    ssq = _split_dot(kkr * kkr, bd_ref[...])
    kkn = kkr / jnp.maximum(jnp.sqrt(ssq), 1e-12)

    r_o[...] = r
    v_o[...] = v
    kkn_o[...] = kkn
    g_o[...] = g
    zp_o[...] = p[:, OFF_POOL:]
    ka = ka_ref[...]
    for d in range(2):
        a_d = a[:, d * D_RWKV:(d + 1) * D_RWKV]
        w_o[d] = decay[:, d * D_RWKV:(d + 1) * D_RWKV]
        kx_o[d] = k * (1.0 + (a_d - 1.0) * ka)
        b_o[d] = kkn * a_d


def _pair_spec(tiles_per_seq, lead):
    zeros = (0,) * lead
    return pl.BlockSpec(
        (2,) * lead + (TM, D_RWKV),
        lambda i: zeros + (i % tiles_per_seq, i // tiles_per_seq))


def _in_call(x, mod3, mod_row, tiles_per_seq, g1, win, mu, w0, wup, a0, aup, gup, kk_k, k_a, bd):
    n = x.shape[0]
    nt = n // TM
    nb8 = n // 8
    t = tiles_per_seq * TM
    b = n // t
    const = lambda i: (0, 0)
    tok = lambda i: (i, 0)
    f512 = jax.ShapeDtypeStruct((n, D_RWKV), F32)
    p1 = jax.ShapeDtypeStruct((t, b * D_RWKV), F32)
    p2 = jax.ShapeDtypeStruct((2, t, b * D_RWKV), F32)
    ps1 = _pair_spec(tiles_per_seq, 0)
    ps2 = _pair_spec(tiles_per_seq, 1)
    return pl.pallas_call(
        functools.partial(_in_kernel, tiles_per_seq=tiles_per_seq),
        grid=(nt,),
        in_specs=[
            pl.BlockSpec((TM, D), tok),
            pl.BlockSpec((8, D), lambda i: (jnp.maximum(i * (TM // 8) - 1, 0), 0)),
            pl.BlockSpec((8, D), lambda i: (jnp.minimum((i + 1) * (TM // 8), nb8 - 1), 0)),
            pl.BlockSpec((1, 1, N_MOD * D), lambda i: (mod_row(i), 0, 0)),
            pl.BlockSpec((1, D), const),
            pl.BlockSpec((D, D_IN), const),
            pl.BlockSpec((1, OFF_POOL), const),
            pl.BlockSpec((1, 2 * D_RWKV), const),
            pl.BlockSpec((128, 2 * D_RWKV), const),
            pl.BlockSpec((1, 2 * D_RWKV), const),
            pl.BlockSpec((128, 2 * D_RWKV), const),
            pl.BlockSpec((128, D_RWKV), const),
            pl.BlockSpec((1, D_RWKV), const),
            pl.BlockSpec((1, D_RWKV), const),
            pl.BlockSpec((D_RWKV, D_RWKV), const),
        ],
        out_specs=[ps1, ps1, ps1, pl.BlockSpec((TM, D_RWKV), tok), ps2, ps2, ps2,
                   pl.BlockSpec((TM, D_RWKV), tok)],
        out_shape=[p1, p1, p1, f512, p2, p2, p2, f512],
        compiler_params=_cparams("arbitrary"),
        name="in_proj",
    )(x, x, x, mod3, g1, win, mu, w0, wup, a0, aup, gup, kk_k, k_a, bd)


SCAN_SETS = 4
KK, W, B, KX, R, V = range(6)


def _scan_fill(dst, srcs, i, ng):
    ib = TB - 1 - i
    bases = [jnp.concatenate([f_ref[lead + (i,)], b_ref[lead + (ib,)]], axis=0)
             for (f_ref, b_ref, lead) in srcs]
    if ng == 2:
        for a in range(6):
            dst[a][...] = bases[a].T
        return
    low = lax.broadcasted_iota(jnp.int32, bases[0].shape, 1) < HEAD

    def pack(x, y):
        even = jnp.where(low, x, pltpu.roll(y, 64, 1))
        odd = jnp.where(low, pltpu.roll(x, 64, 1), y)
        return jnp.concatenate([even, odd, even, odd], axis=0)

    dst[0][...] = pack(bases[KK], bases[W]).T
    dst[1][...] = pack(bases[B], bases[KX]).T
    r, v = bases[R], bases[V]
    rs = pltpu.roll(r, 64, 1)
    tile = jnp.concatenate([jnp.where(low, r, pltpu.roll(v, 64, 1)),
                            jnp.where(low, rs, v),
                            jnp.where(low, r, pltpu.roll(v, 32, 1)),
                            jnp.where(low, rs, pltpu.roll(v, 96, 1))], axis=0)
    dst[2][...] = tile.T


def _scan_row(src, ng, name, g, k):
    if ng == 2:
        return src[name][pl.ds(g * HEAD + k, 1), :]
    return src[name // 2][pl.ds((name % 2) * HEAD + k, 1), :]


def _scan_value(src, ng, g, vp):
    if ng == 2:
        return src[V][g * vp:(g + 1) * vp, :]
    return src[V // 2][HEAD:HEAD + vp, :]


def _scan_sa(src, s_ref, ng, vp):
    out = []
    for g in range(ng):
        acc = jnp.zeros((vp, LANES), F32)
        for k in range(HEAD):
            acc = acc + s_ref[g, k] * _scan_row(src, ng, KK, g, k)
        out.append(acc)
    return out


def _scan_step(src, nxt, s_ref, yraw_ref, i, sa, ng, vp):
    sa_next = []
    for g in range(ng):
        vv = _scan_value(src, ng, g, vp)
        y = jnp.zeros((vp, LANES), F32)
        acc = jnp.zeros((vp, LANES), F32)
        for k in range(HEAD):
            sn = (s_ref[g, k] * _scan_row(src, ng, W, g, k) - sa[g] * _scan_row(src, ng, B, g, k)
                  + vv * _scan_row(src, ng, KX, g, k))
            s_ref[g, k] = sn
            y = y + sn * _scan_row(src, ng, R, g, k)
            acc = acc + sn * _scan_row(nxt, ng, KK, g, k)
        yraw_ref[i, g * vp:(g + 1) * vp, :] = y
        sa_next.append(acc)
    return sa_next


def _scan_emit(yraw_ref, yf_ref, yb_ref, i, ng, vp):
    ib = TB - 1 - i
    if ng == 2:
        yt = yraw_ref[i].T
    else:
        sq = jnp.concatenate([yraw_ref[i], jnp.zeros((LANES - vp, LANES), F32)], axis=0).T
        q = LANES // 4
        yt = (sq[0:q] + pltpu.roll(sq[q:2 * q], 64, 1) + pltpu.roll(sq[2 * q:3 * q], 32, 1)
              + pltpu.roll(sq[3 * q:4 * q], 96, 1))
    half = yt.shape[0] // 2
    yf_ref[i] = yt[0:half]
    yb_ref[ib] = yt[half:]


SCAN_EMIT_UNROLL = 8


def _scan_kernel(kkf, kkb, rf, rb, vf, vb, wf, wb, bf, bb, kxf, kxb, s0_ref, yf_ref, yb_ref, sfin_ref,
                 s_ref, yraw_ref, *scr, ng, vp):
    tb = pl.program_id(0)

    @pl.when(tb == 0)
    def _():
        s_ref[...] = s0_ref[...]

    srcs = [(kkf, kkb, ()), (wf, wb, (0,)), (bf, bb, (0,)), (kxf, kxb, (0,)), (rf, rb, ()), (vf, vb, ())]
    per_set = len(scr) // SCAN_SETS
    sets = [scr[q * per_set:(q + 1) * per_set] for q in range(SCAN_SETS)]
    _scan_fill(sets[0], srcs, 0, ng)
    _scan_fill(sets[1], srcs, 1, ng)

    def group(j, sa):
        i0 = SCAN_SETS * j
        sa = list(sa)
        for s in range(SCAN_SETS):
            _scan_fill(sets[(s + 2) % SCAN_SETS], srcs, jnp.minimum(i0 + s + 2, TB - 1), ng)
            sa = _scan_step(sets[s], sets[(s + 1) % SCAN_SETS], s_ref, yraw_ref, i0 + s, sa, ng, vp)
        return tuple(sa)

    lax.fori_loop(0, TB // SCAN_SETS, group, tuple(_scan_sa(sets[0], s_ref, ng, vp)))

    def emit(j, carry):
        for u in range(SCAN_EMIT_UNROLL):
            _scan_emit(yraw_ref, yf_ref, yb_ref, SCAN_EMIT_UNROLL * j + u, ng, vp)
        return carry

    lax.fori_loop(0, TB // SCAN_EMIT_UNROLL, emit, 0)

    @pl.when(tb == pl.num_programs(0) - 1)
    def _():
        sfin_ref[...] = s_ref[...]


def _scan_call(kk, r, v, w, bb, kx, s0):
    t, rows, _ = kk.shape
    ng, _, vp, _ = s0.shape
    nt = t // TB
    f1 = pl.BlockSpec((TB, rows, LANES), lambda ti: (ti, 0, 0))
    b1 = pl.BlockSpec((TB, rows, LANES), lambda ti: (nt - 1 - ti, 0, 0))
    f2 = pl.BlockSpec((1, TB, rows, LANES), lambda ti: (0, ti, 0, 0))
    b2 = pl.BlockSpec((1, TB, rows, LANES), lambda ti: (1, nt - 1 - ti, 0, 0))
    sspec = pl.BlockSpec((ng, HEAD, vp, LANES), lambda ti: (0, 0, 0, 0))
    tiles_per_set = 6 if ng == 2 else 3
    scratch = [pltpu.VMEM((ng, HEAD, vp, LANES), F32), pltpu.VMEM((TB, ng * vp, LANES), F32)]
    scratch += [pltpu.VMEM((LANES, LANES), F32)] * (SCAN_SETS * tiles_per_set)
    ysd = jax.ShapeDtypeStruct((t, rows, LANES), F32)
    return pl.pallas_call(
        functools.partial(_scan_kernel, ng=ng, vp=vp),
        grid=(nt,),
        in_specs=[f1, b1, f1, b1, f1, b1, f2, b2, f2, b2, f2, b2, sspec],
        out_specs=[f1, b1, sspec],
        out_shape=[ysd, ysd, jax.ShapeDtypeStruct((ng, HEAD, vp, LANES), F32)],
        scratch_shapes=scratch,
        compiler_params=_cparams("arbitrary"),
        name="wkv_scan",
    )(kk, kk, r, r, v, v, w, w, bb, bb, kx, kx, s0)


def _pool_seq_kernel(z_ref, a_ref, inv_ref, pw_ref, ps_ref, o_ref):
    for gi in range(len(POOL_WINDOWS)):
        sl = slice(gi * POOL_GROUP, (gi + 1) * POOL_GROUP)
        zg = z_ref[:, sl]
        s = _split_dot_left(a_ref[gi], zg)
        d = s * inv_ref[:, sl] - zg
        o_ref[:, sl] = _dot(d.astype(BF16), pw_ref[gi]) * ps_ref[:, sl]


def _pool_seq_call(z, t, amat, inv, pw, ps):
    n = z.shape[0]
    ng = len(POOL_WINDOWS)
    return pl.pallas_call(
        _pool_seq_kernel,
        grid=(n // t,),
        in_specs=[
            pl.BlockSpec((t, D_RWKV), lambda i: (i, 0)),
            pl.BlockSpec((ng, t, t), lambda i: (0, 0, 0)),
            pl.BlockSpec((t, D_RWKV), lambda i: (0, 0)),
            pl.BlockSpec((ng, POOL_GROUP, POOL_GROUP), lambda i: (0, 0, 0)),
            pl.BlockSpec((1, D_RWKV), lambda i: (0, 0)),
        ],
        out_specs=pl.BlockSpec((t, D_RWKV), lambda i: (i, 0)),
        out_shape=jax.ShapeDtypeStruct((n, D_RWKV), F32),
        compiler_params=_cparams("arbitrary"),
        name="pool_seq",
    )(z, amat, inv, pw, ps)


POOL_BLK = 256
POOL_PAD = (max(POOL_WINDOWS) // 2) * GRID_W


def _pool_grid_kernel(z_ref, a_ref, inv_ref, pw_ref, ps_ref, o_ref, pad_ref, *, t):
    zeros = jnp.zeros((POOL_PAD, POOL_GROUP), F32)
    pad_ref[0:POOL_PAD, :] = zeros
    pad_ref[POOL_PAD + t:POOL_PAD + t + POOL_PAD, :] = zeros
    for gi, win in enumerate(POOL_WINDOWS):
        sl = slice(gi * POOL_GROUP, (gi + 1) * POOL_GROUP)
        for blk in range(t // POOL_BLK):
            zb = z_ref[blk * POOL_BLK:(blk + 1) * POOL_BLK, sl]
            pad_ref[POOL_PAD + blk * POOL_BLK:POOL_PAD + (blk + 1) * POOL_BLK, :] = (
                _split_dot_left(a_ref[gi], zb))
        lo = win // 2
        hi = win - 1 - lo
        s = pad_ref[POOL_PAD - lo * GRID_W:POOL_PAD - lo * GRID_W + t, :]
        for j in range(-lo + 1, hi + 1):
            s = s + pad_ref[POOL_PAD + j * GRID_W:POOL_PAD + j * GRID_W + t, :]
        d = s * inv_ref[:, sl] - z_ref[:, sl]
        o_ref[:, sl] = _dot(d.astype(BF16), pw_ref[gi]) * ps_ref[:, sl]


def _pool_grid_call(z, t, amat, inv, pw, ps):
    n = z.shape[0]
    ng = len(POOL_WINDOWS)
    return pl.pallas_call(
        functools.partial(_pool_grid_kernel, t=t),
        grid=(n // t,),
        in_specs=[
            pl.BlockSpec((t, D_RWKV), lambda i: (i, 0)),
            pl.BlockSpec((ng, POOL_BLK, POOL_BLK), lambda i: (0, 0, 0)),
            pl.BlockSpec((t, D_RWKV), lambda i: (0, 0)),
            pl.BlockSpec((ng, POOL_GROUP, POOL_GROUP), lambda i: (0, 0, 0)),
            pl.BlockSpec((1, D_RWKV), lambda i: (0, 0)),
        ],
        out_specs=pl.BlockSpec((t, D_RWKV), lambda i: (i, 0)),
        out_shape=jax.ShapeDtypeStruct((n, D_RWKV), F32),
        scratch_shapes=[pltpu.VMEM((t + 2 * POOL_PAD, POOL_GROUP), F32)],
        compiler_params=_cparams("arbitrary"),
        name="pool_grid",
    )(z, amat, inv, pw, ps)


def _band(n, win):
    lo = win // 2
    hi = win - 1 - lo
    pos = np.arange(n)
    start = np.clip(pos - lo, 0, n)
    end = np.clip(pos + hi + 1, 0, n)
    m = (pos[None, :] >= start[:, None]) & (pos[None, :] < end[:, None])
    return m.astype(np.float32), (end - start).astype(np.float32)


def _pool_constants(t, use_grid):
    mats, invs = [], []
    for win in POOL_WINDOWS:
        if use_grid:
            rows = t // GRID_W
            mcol, ccol = _band(GRID_W, win)
            _, crow = _band(rows, win)
            mats.append(np.kron(np.eye(POOL_BLK // GRID_W, dtype=np.float32), mcol))
            cnt = (crow[:, None] * ccol[None, :]).reshape(t)
        else:
            m, cnt = _band(t, win)
            mats.append(m)
        invs.append(np.repeat((1.0 / cnt)[:, None], POOL_GROUP, axis=1))
    return (jnp.asarray(np.stack(mats), BF16),
            jnp.asarray(np.concatenate(invs, axis=1), F32))


def _post_kernel(x_ref, yf_ref, yb_ref, r_ref, v_ref, g_ref, kx_ref, yp_ref, mod_ref, lnw_ref,
                 lnb_ref, rk_ref, bd_ref, wout_ref, g2_ref, wrh_ref, wrl_ref,
                 x1_o, h2_o, pt_o):
    mod = mod_ref[0]
    gt1 = mod[:, 2 * D:3 * D]
    sh2 = mod[:, 3 * D:4 * D]
    sc2 = mod[:, 4 * D:5 * D]
    bd = bd_ref[...]
    inv_n = np.float32(1.0 / HEAD)

    y = yf_ref[...] + yb_ref[...]
    mu = _split_dot(y, bd) * inv_n
    yc = y - mu
    var = _split_dot(yc * yc, bd) * inv_n
    yn = yc * lax.rsqrt(var + GN_EPS) * lnw_ref[...] + lnb_ref[...]
    v = v_ref[...]
    bonus = _split_dot(r_ref[...] * (kx_ref[0] + kx_ref[1]) * rk_ref[...], bd) * v
    yr = ((yn + bonus) * g_ref[...]).astype(BF16)
    m = _dot(yr, wout_ref[0:D_RWKV, :]) + _dot(yp_ref[...].astype(BF16), wout_ref[D_RWKV:D, :])
    x1 = x_ref[...] + gt1 * m
    x1_o[...] = x1
    h = x1 * lax.rsqrt(jnp.mean(x1 * x1, axis=-1, keepdims=True) + RMS_EPS)
    h2 = (h * g2_ref[...]) * (1.0 + sc2) + sh2
    h2_o[...] = h2.astype(BF16)

    hi = h2.astype(BF16)
    lo = (h2 - hi.astype(F32)).astype(BF16)
    dn = (((1,), (1,)), ((), ()))
    logits = (lax.dot_general(wrh_ref[...], hi, dn, preferred_element_type=F32)
              + lax.dot_general(wrh_ref[...], lo, dn, preferred_element_type=F32)
              + lax.dot_general(wrl_ref[...], hi, dn, preferred_element_type=F32))
    mx = jnp.max(logits, axis=0, keepdims=True)
    e = jnp.exp(logits - mx)
    pt_o[...] = e / jnp.sum(e, axis=0, keepdims=True)


def _post_call(x, yf, yb, r, v, g, kx, yp, mod3, mod_row, tiles_per_seq, lnw, lnb, rk, bd, wout, g2,
               wrh, wrl):
    n = x.shape[0]
    const = lambda i: (0, 0)
    tok = lambda i: (i, 0)
    t512 = pl.BlockSpec((TM, D_RWKV), tok)
    v512 = pl.BlockSpec((1, D_RWKV), const)
    ps1 = _pair_spec(tiles_per_seq, 0)
    return pl.pallas_call(
        _post_kernel,
        grid=(n // TM,),
        in_specs=[
            pl.BlockSpec((TM, D), tok), ps1, ps1, ps1, ps1, t512,
            _pair_spec(tiles_per_seq, 1),
            t512,
            pl.BlockSpec((1, 1, N_MOD * D), lambda i: (mod_row(i), 0, 0)),
            v512, v512, v512,
            pl.BlockSpec((D_RWKV, D_RWKV), const),
            pl.BlockSpec((D, D), const),
            pl.BlockSpec((1, D), const),
            pl.BlockSpec((N_EXPERTS, D), const),
            pl.BlockSpec((N_EXPERTS, D), const),
        ],
        out_specs=[
            pl.BlockSpec((TM, D), tok),
            pl.BlockSpec((TM, D), tok),
            pl.BlockSpec((N_EXPERTS, TM), lambda i: (0, i)),
        ],
        out_shape=[jax.ShapeDtypeStruct((n, D), F32),
                   jax.ShapeDtypeStruct((n, D), BF16),
                   jax.ShapeDtypeStruct((N_EXPERTS, n), F32)],
        compiler_params=_cparams("arbitrary"),
        name="post_mix",
    )(x, yf, yb, r, v, g, kx, yp, mod3, lnw, lnb, rk, bd, wout, g2, wrh, wrl)


MOE_TF = 256


def _moe_kernel(xc_ref, xs_ref, gc_ref, gs_ref, w1_ref, w3_ref, w2_ref, oc_ref, os_ref):
    j = pl.program_id(1)
    cc = xc_ref.shape[1]
    xe = jnp.concatenate([xc_ref[0], xs_ref[0]], axis=0)
    h1 = _dot(xe, w1_ref[0, 0].astype(BF16))
    h3 = _dot(xe, w3_ref[0, 0].astype(BF16))
    he = (h1 * _sigmoid(h1) * h3).astype(BF16)
    part = _dot(he, w2_ref[0, 0].astype(BF16))

    @pl.when(j == 0)
    def _():
        oc_ref[0] = part[:cc]
        os_ref[0] = part[cc:]

    @pl.when(j > 0)
    def _():
        oc_ref[0] = oc_ref[0] + part[:cc]
        os_ref[0] = os_ref[0] + part[cc:]

    @pl.when(j == pl.num_programs(1) - 1)
    def _():
        oc_ref[0] = oc_ref[0] * gc_ref[0]
        os_ref[0] = os_ref[0] * gs_ref[0]


def _moe_call(xc, xs, gc, gs, w1, w3, w2, layer):
    e, cc, _ = xc.shape
    cs = xs.shape[1]
    per_expert = lambda c, d: pl.BlockSpec((1, c, d), lambda ei, j: (ei, 0, 0))
    return pl.pallas_call(
        _moe_kernel,
        grid=(e, D_EXPERT // MOE_TF),
        in_specs=[
            per_expert(cc, D), per_expert(cs, D), per_expert(cc, 1), per_expert(cs, 1),
            pl.BlockSpec((1, 1, D, MOE_TF), lambda ei, j: (layer, ei, 0, j)),
            pl.BlockSpec((1, 1, D, MOE_TF), lambda ei, j: (layer, ei, 0, j)),
            pl.BlockSpec((1, 1, MOE_TF, D), lambda ei, j: (layer, ei, j, 0)),
        ],
        out_specs=[per_expert(cc, D), per_expert(cs, D)],
        out_shape=[jax.ShapeDtypeStruct((e, cc, D), F32), jax.ShapeDtypeStruct((e, cs, D), F32)],
        compiler_params=_cparams("arbitrary", "arbitrary"),
        name="moe_ffn",
    )(xc, xs, gc, gs, w1, w3, w2)


def _final_kernel(x_ref, g_ref, o_ref):
    x = x_ref[...]
    o_ref[...] = x * lax.rsqrt(jnp.mean(x * x, axis=-1, keepdims=True) + RMS_EPS) * g_ref[...]


def _final_call(x, gain):
    n = x.shape[0]
    tm = 512
    return pl.pallas_call(
        _final_kernel,
        grid=(n // tm,),
        in_specs=[pl.BlockSpec((tm, D), lambda i: (i, 0)), pl.BlockSpec((1, D), lambda i: (0, 0))],
        out_specs=pl.BlockSpec((tm, D), lambda i: (i, 0)),
        out_shape=jax.ShapeDtypeStruct((n, D), F32),
        compiler_params=_cparams("arbitrary"),
        name="final_norm",
    )(x, gain)


def _scan_pass(r, v, kkn, w, kx, bb, s0):
    t, width = r.shape
    rows = width // LANES
    flat = lambda u: u.reshape(u.shape[:-1] + (rows, LANES))
    yf, yb, sfin = _scan_call(flat(kkn), flat(r), flat(v), flat(w), flat(bb), flat(kx), s0)
    return yf.reshape(t, width), yb.reshape(t, width), sfin


def _state_to_lanes(s):
    b = s.shape[0]
    hv = HEAD // 2
    return (s.reshape(b, 2, N_PAIRS, 2, 2, hv, HEAD).transpose(6, 5, 4, 3, 1, 0, 2)
            .reshape(1, HEAD, hv, LANES))


def _state_from_lanes(s, b):
    return (s.reshape(2, HEAD, HEAD, 2, b, N_PAIRS).transpose(4, 3, 5, 0, 2, 1)
            .reshape(b, 2, N_HEADS, HEAD, HEAD))


def kernel(x_prompt, x_sample, state_rwkv, c, c_ctx, norm1, norm2, norm_f, ada_w, ada_b, w_in, mu_shift, w0, w_up, a0, a_up, g_up, k_k, k_a, r_k, ln_x_w, ln_x_b, pool_w, pool_scale, w_out, w_router, w1, w3, w2):
    bc, tc, _ = x_prompt.shape
    bs, ts, _ = x_sample.shape
    nc = bc * tc
    ns = bs * ts
    cap_c = 2 * nc // N_EXPERTS
    cap_s = 2 * ns // N_EXPERTS

    cv = jnp.concatenate([c_ctx[None, :], c, jnp.zeros((8 - 1 - bs, D), F32)], axis=0)
    mod_all = _mod_call(cv, ada_w, ada_b)

    win_b = w_in.astype(BF16)
    wout_b = w_out.astype(BF16)
    gup_b = g_up.astype(BF16)
    pw_b = pool_w.astype(BF16)
    zeros_up = jnp.zeros((DEPTH, HEAD, D_RWKV), F32)

    def blockdiag(u):
        top = jnp.concatenate([u[:, 0], zeros_up], axis=-1)
        bot = jnp.concatenate([zeros_up, u[:, 1]], axis=-1)
        return jnp.concatenate([top, bot], axis=1).astype(BF16)

    wup_b = blockdiag(w_up)
    aup_b = blockdiag(a_up)
    w0_f = w0.reshape(DEPTH, 1, 2 * D_RWKV)
    a0_f = a0.reshape(DEPTH, 1, 2 * D_RWKV)
    wr_t = jnp.swapaxes(w_router, 1, 2)
    wr_hi = wr_t.astype(BF16)
    wr_lo = (wr_t - wr_hi.astype(F32)).astype(BF16)
    bd = jnp.asarray(np.kron(np.eye(N_HEADS, dtype=np.float32), np.ones((HEAD, HEAD), np.float32)), BF16)

    amat_c, inv_c = _pool_constants(tc, False)
    amat_s, inv_s = _pool_constants(ts, True)

    row_c = lambda i: 0
    tiles_s = ts // TM
    row_s = lambda i: 1 + i // tiles_s

    xc = x_prompt.reshape(nc, D)
    xs = x_sample.reshape(ns, D)
    ctx_states = []
    zero_state = jnp.zeros((2, HEAD, HEAD, LANES), F32)
    for l in range(DEPTH):
        mod3 = mod_all[l].reshape(8, 1, N_MOD * D)
        row1 = lambda u: u[l].reshape(1, -1)
        lay_in = (row1(norm1), win_b[l], row1(mu_shift), w0_f[l], wup_b[l], a0_f[l], aup_b[l], gup_b[l],
                  row1(k_k), row1(k_a), bd)
        lay_post = (row1(ln_x_w), row1(ln_x_b), row1(r_k), bd, wout_b[l], row1(norm2), wr_hi[l], wr_lo[l])

        r, v, kkn, g, w, kx, bb, zp = _in_call(xc, mod3, row_c, 1, *lay_in)
        yf, yb, sfin = _scan_pass(r, v, kkn, w, kx, bb, zero_state)
        ctx_states.append(_state_from_lanes(sfin, bc))
        yp = _pool_seq_call(zp, tc, amat_c, inv_c, pw_b[l], row1(pool_scale))
        x1c, h2c, ptc = _post_call(xc, yf, yb, r, v, g, kx, yp, mod3, row_c, 1, *lay_post)

        r, v, kkn, g, w, kx, bb, zp = _in_call(xs, mod3, row_s, tiles_s, *lay_in)
        yf, yb, _ = _scan_pass(r, v, kkn, w, kx, bb, _state_to_lanes(state_rwkv[:, l]))
        yp = _pool_grid_call(zp, ts, amat_s, inv_s, pw_b[l], row1(pool_scale))
        x1s, h2s, pts = _post_call(xs, yf, yb, r, v, g, kx, yp, mod3, row_s, tiles_s, *lay_post)

        gate_c, idx_c = lax.top_k(ptc, cap_c)
        gate_s, idx_s = lax.top_k(pts, cap_s)
        ye_c, ye_s = _moe_call(h2c[idx_c], h2s[idx_s], gate_c[..., None], gate_s[..., None], w1, w3, w2, l)
        out_c = jnp.zeros((nc, D), F32).at[idx_c.reshape(-1)].add(ye_c.reshape(-1, D))
        out_s = jnp.zeros((ns, D), F32).at[idx_s.reshape(-1)].add(ye_s.reshape(-1, D))
        gt2_c = mod_all[l, 0, 5 * D:][None, :]
        gt2_s = jnp.repeat(mod_all[l, 1:1 + bs, 5 * D:], ts, axis=0)
        xc = x1c + gt2_c * out_c
        xs = x1s + gt2_s * out_s

    gf = norm_f.reshape(1, D)
    y_prompt = _final_call(xc, gf).reshape(bc, tc, D)
    y_sample = _final_call(xs, gf).reshape(bs, ts, D)
    new_state = jnp.stack(ctx_states, axis=1)
    return (y_prompt, y_sample, new_state)
```

```python
import functools

import jax
import jax.numpy as jnp
import numpy as np
from jax import lax
from jax.experimental import pallas as pl
from jax.experimental.pallas import tpu as pltpu

F32 = jnp.float32
BF16 = jnp.bfloat16

D = 1024
DEPTH = 4
GRID_W = 64
D_RWKV = 512
HEAD = 64
N_HEADS = 8
POOL_WINDOWS = (2, 4, 8, 16)
POOL_GROUP = 128
N_EXPERTS = 16
D_EXPERT = 2048
N_MOD = 6
RMS_EPS = 1e-6
GN_EPS = 64e-5
OFF_K = 512
OFF_V = 1024
OFF_G = 1536
OFF_W = 1664
OFF_A = 1792
OFF_POOL = 1920
D_IN = 2432

VMEM_LIMIT = 52 * 1024 * 1024
TM = 256
TB = 32
LANES = 128


def _cparams(*sem):
    return pltpu.CompilerParams(dimension_semantics=sem, vmem_limit_bytes=VMEM_LIMIT)


def _sigmoid(x):
    return 1.0 / (1.0 + jnp.exp(-x))


def _dot(a, b):
    return jnp.dot(a, b, preferred_element_type=F32)


def _split_dot(x, m):
    hi = x.astype(BF16)
    lo = (x - hi.astype(F32)).astype(BF16)
    return _dot(hi, m) + _dot(lo, m)


def _split_dot_left(m, x):
    hi = x.astype(BF16)
    lo = (x - hi.astype(F32)).astype(BF16)
    return _dot(m, hi) + _dot(m, lo)


N_PAIRS = N_HEADS // 2


def _mod_kernel(cv_ref, w_ref, b_ref, o_ref):
    cv = cv_ref[...]
    s = (cv * _sigmoid(cv)).astype(BF16)
    o_ref[0] = _dot(s, w_ref[0].astype(BF16)) + b_ref[0]


def _mod_call(cv, ada_w, ada_b):
    tn = 1536
    n_out = N_MOD * D
    return pl.pallas_call(
        _mod_kernel,
        grid=(DEPTH, n_out // tn),
        in_specs=[
            pl.BlockSpec((8, D), lambda l, j: (0, 0)),
            pl.BlockSpec((1, D, tn), lambda l, j: (l, 0, j)),
            pl.BlockSpec((1, 1, tn), lambda l, j: (l, 0, j)),
        ],
        out_specs=pl.BlockSpec((1, 8, tn), lambda l, j: (l, 0, j)),
        out_shape=jax.ShapeDtypeStruct((DEPTH, 8, n_out), F32),
        compiler_params=_cparams("arbitrary", "arbitrary"),
        name="adaln_mod",
    )(cv, ada_w, ada_b.reshape(DEPTH, 1, n_out))


SUB = 2


def _tm_store(o_ref, lead, h, wide, val):
    if wide:
        o_ref[lead + (slice(None), slice(h * D_RWKV, (h + 1) * D_RWKV))] = val
    else:
        o_ref[lead + (slice(h * TM, (h + 1) * TM), slice(None))] = val


def _tm_load(ref, lead, h, wide):
    if wide:
        return ref[lead + (slice(None), slice(h * D_RWKV, (h + 1) * D_RWKV))]
    return ref[lead + (slice(h * TM, (h + 1) * TM), slice(None))]


def _in_kernel(x_ref, xp_ref, xn_ref, mod_ref, g1_ref, win_ref, mu_ref, w0_ref, wup_ref, a0_ref,
               aup_ref, gup_ref, kk_ref, ka_ref, bd_ref,
               r_o, v_o, kkn_o, g_o, w_o, kx_o, b_o, zp_o, *, tiles_per_seq, wide):
    mod = mod_ref[0]
    sh1 = mod[:, 0:D]
    sc1 = mod[:, D:2 * D]
    g1 = g1_ref[...]

    def hfun(x):
        y = x * lax.rsqrt(jnp.mean(x * x, axis=-1, keepdims=True) + RMS_EPS)
        return ((y * g1) * (1.0 + sc1) + sh1).astype(BF16)

    for h in range(SUB):
        tok = slice(h * TM, (h + 1) * TM)
        p = _dot(hfun(x_ref[tok, :]), win_ref[...])
        z = p[:, :OFF_POOL]
        rows = lax.broadcasted_iota(jnp.int32, z.shape, 0)
        prev = pltpu.roll(z, 1, 0)
        nxt = pltpu.roll(z, TM - 1, 0)
        if tiles_per_seq > 1:
            j = (pl.program_id(0) * SUB + h) % tiles_per_seq
            wz = win_ref[:, :OFF_POOL]
            x_before = xp_ref[...] if h == 0 else x_ref[h * TM - 8:h * TM, :]
            x_after = xn_ref[...] if h == SUB - 1 else x_ref[(h + 1) * TM:(h + 1) * TM + 8, :]
            pp = _dot(hfun(x_before), wz)
            pn = _dot(hfun(x_after), wz)
            prow = jnp.where(j > 0, pp[7:8, :], 0.0)
            nrow = jnp.where(j < tiles_per_seq - 1, pn[0:1, :], 0.0)
        else:
            prow = 0.0
            nrow = 0.0
        prev = jnp.where(rows == 0, prow, prev)
        nxt = jnp.where(rows == TM - 1, nrow, nxt)
        zs = z + mu_ref[...] * (0.5 * (prev + nxt) - z)

        r = zs[:, :OFF_K]
        k = zs[:, OFF_K:OFF_V]
        v = zs[:, OFF_V:OFF_G]
        gd = zs[:, OFF_G:OFF_W]
        wd = zs[:, OFF_W:OFF_A]
        ad = zs[:, OFF_A:OFF_POOL]

        wl = w0_ref[...] + _dot(jnp.tanh(wd).astype(BF16), wup_ref[...])
        decay = jnp.exp(-np.float32(np.exp(-0.5)) * _sigmoid(wl))
        a = _sigmoid(a0_ref[...] + _dot(ad.astype(BF16), aup_ref[...]))
        g = _dot(_sigmoid(gd).astype(BF16), gup_ref[...])

        kkr = k * kk_ref[...]
        ssq = _split_dot(kkr * kkr, bd_ref[...])
        kkn = kkr / jnp.maximum(jnp.sqrt(ssq), 1e-12)

        _tm_store(r_o, (), h, wide, r)
        _tm_store(v_o, (), h, wide, v)
        _tm_store(kkn_o, (), h, wide, kkn)
        g_o[tok, :] = g
        zp_o[tok, :] = p[:, OFF_POOL:]
        ka = ka_ref[...]
        for d in range(2):
            a_d = a[:, d * D_RWKV:(d + 1) * D_RWKV]
            _tm_store(w_o, (d,), h, wide, decay[:, d * D_RWKV:(d + 1) * D_RWKV])
            _tm_store(kx_o, (d,), h, wide, k * (1.0 + (a_d - 1.0) * ka))
            _tm_store(b_o, (d,), h, wide, kkn * a_d)


def _tm_spec(t, lead):
    zeros = (0,) * lead
    if t == TM:
        return pl.BlockSpec((2,) * lead + (TM, SUB * D_RWKV), lambda i: zeros + (0, i))
    steps = t // (SUB * TM)
    return pl.BlockSpec((2,) * lead + (SUB * TM, D_RWKV), lambda i: zeros + (i % steps, i // steps))


def _in_call(x, mod3, mod_row, t, g1, win, mu, w0, wup, a0, aup, gup, kk_k, k_a, bd):
    n = x.shape[0]
    tg = SUB * TM
    nb8 = n // 8
    b = n // t
    const = lambda i: (0, 0)
    tok = lambda i: (i, 0)
    f512 = jax.ShapeDtypeStruct((n, D_RWKV), F32)
    p1 = jax.ShapeDtypeStruct((t, b * D_RWKV), F32)
    p2 = jax.ShapeDtypeStruct((2, t, b * D_RWKV), F32)
    ps1 = _tm_spec(t, 0)
    ps2 = _tm_spec(t, 1)
    return pl.pallas_call(
        functools.partial(_in_kernel, tiles_per_seq=t // TM, wide=(t == TM)),
        grid=(n // tg,),
        in_specs=[
            pl.BlockSpec((tg, D), tok),
            pl.BlockSpec((8, D), lambda i: (jnp.maximum(i * (tg // 8) - 1, 0), 0)),
            pl.BlockSpec((8, D), lambda i: (jnp.minimum((i + 1) * (tg // 8), nb8 - 1), 0)),
            pl.BlockSpec((1, 1, N_MOD * D), lambda i: (mod_row(i), 0, 0)),
            pl.BlockSpec((1, D), const),
            pl.BlockSpec((D, D_IN), const),
            pl.BlockSpec((1, OFF_POOL), const),
            pl.BlockSpec((1, 2 * D_RWKV), const),
            pl.BlockSpec((128, 2 * D_RWKV), const),
            pl.BlockSpec((1, 2 * D_RWKV), const),
            pl.BlockSpec((128, 2 * D_RWKV), const),
            pl.BlockSpec((128, D_RWKV), const),
            pl.BlockSpec((1, D_RWKV), const),
            pl.BlockSpec((1, D_RWKV), const),
            pl.BlockSpec((D_RWKV, D_RWKV), const),
        ],
        out_specs=[ps1, ps1, ps1, pl.BlockSpec((tg, D_RWKV), tok), ps2, ps2, ps2,
                   pl.BlockSpec((tg, D_RWKV), tok)],
        out_shape=[p1, p1, p1, f512, p2, p2, p2, f512],
        compiler_params=_cparams("arbitrary"),
        name="in_proj",
    )(x, x, x, mod3, g1, win, mu, w0, wup, a0, aup, gup, kk_k, k_a, bd)


SCAN_SETS = 4
KK, W, B, KX, R, V = range(6)


def _scan_fill(dst, srcs, i, ng):
    ib = TB - 1 - i
    bases = [jnp.concatenate([f_ref[lead + (i,)], b_ref[lead + (ib,)]], axis=0)
             for (f_ref, b_ref, lead) in srcs]
    if ng == 2:
        for a in range(6):
            dst[a][...] = bases[a].T
        return
    low = lax.broadcasted_iota(jnp.int32, bases[0].shape, 1) < HEAD

    def pack(x, y):
        even = jnp.where(low, x, pltpu.roll(y, 64, 1))
        odd = jnp.where(low, pltpu.roll(x, 64, 1), y)
        return jnp.concatenate([even, odd, even, odd], axis=0)

    dst[0][...] = pack(bases[KK], bases[W]).T
    dst[1][...] = pack(bases[B], bases[KX]).T
    r, v = bases[R], bases[V]
    rs = pltpu.roll(r, 64, 1)
    tile = jnp.concatenate([jnp.where(low, r, pltpu.roll(v, 64, 1)),
                            jnp.where(low, rs, v),
                            jnp.where(low, r, pltpu.roll(v, 32, 1)),
                            jnp.where(low, rs, pltpu.roll(v, 96, 1))], axis=0)
    dst[2][...] = tile.T


def _scan_row(src, ng, name, g, k):
    if ng == 2:
        return src[name][pl.ds(g * HEAD + k, 1), :]
    return src[name // 2][pl.ds((name % 2) * HEAD + k, 1), :]


def _scan_value(src, ng, g, vp):
    if ng == 2:
        return src[V][g * vp:(g + 1) * vp, :]
    return src[V // 2][HEAD:HEAD + vp, :]


def _scan_sa(src, s_ref, ng, vp):
    out = []
    for g in range(ng):
        acc = jnp.zeros((vp, LANES), F32)
        for k in range(HEAD):
            acc = acc + s_ref[g, k] * _scan_row(src, ng, KK, g, k)
        out.append(acc)
    return out


def _scan_step(src, nxt, s_ref, yraw_ref, i, sa, ng, vp):
    sa_next = []
    for g in range(ng):
        vv = _scan_value(src, ng, g, vp)
        y = jnp.zeros((vp, LANES), F32)
        acc = jnp.zeros((vp, LANES), F32)
        for k in range(HEAD):
            sn = (s_ref[g, k] * _scan_row(src, ng, W, g, k) - sa[g] * _scan_row(src, ng, B, g, k)
                  + vv * _scan_row(src, ng, KX, g, k))
            s_ref[g, k] = sn
            y = y + sn * _scan_row(src, ng, R, g, k)
            acc = acc + sn * _scan_row(nxt, ng, KK, g, k)
        yraw_ref[i, g * vp:(g + 1) * vp, :] = y
        sa_next.append(acc)
    return sa_next


def _scan_emit(yraw_ref, yf_ref, yb_ref, i, ng, vp):
    ib = TB - 1 - i
    if ng == 2:
        yt = yraw_ref[i].T
    else:
        sq = jnp.concatenate([yraw_ref[i], jnp.zeros((LANES - vp, LANES), F32)], axis=0).T
        q = LANES // 4
        yt = (sq[0:q] + pltpu.roll(sq[q:2 * q], 64, 1) + pltpu.roll(sq[2 * q:3 * q], 32, 1)
              + pltpu.roll(sq[3 * q:4 * q], 96, 1))
    half = yt.shape[0] // 2
    yf_ref[i] = yt[0:half]
    yb_ref[ib] = yt[half:]


SCAN_EMIT_UNROLL = 8


def _scan_kernel(kkf, kkb, rf, rb, vf, vb, wf, wb, bf, bb, kxf, kxb, s0_ref, yf_ref, yb_ref, sfin_ref,
                 s_ref, yraw_ref, *scr, ng, vp):
    tb = pl.program_id(0)

    @pl.when(tb == 0)
    def _():
        s_ref[...] = s0_ref[...]

    srcs = [(kkf, kkb, ()), (wf, wb, (0,)), (bf, bb, (0,)), (kxf, kxb, (0,)), (rf, rb, ()), (vf, vb, ())]
    per_set = len(scr) // SCAN_SETS
    sets = [scr[q * per_set:(q + 1) * per_set] for q in range(SCAN_SETS)]
    _scan_fill(sets[0], srcs, 0, ng)
    _scan_fill(sets[1], srcs, 1, ng)

    def group(j, sa):
        i0 = SCAN_SETS * j
        sa = list(sa)
        for s in range(SCAN_SETS):
            _scan_fill(sets[(s + 2) % SCAN_SETS], srcs, jnp.minimum(i0 + s + 2, TB - 1), ng)
            sa = _scan_step(sets[s], sets[(s + 1) % SCAN_SETS], s_ref, yraw_ref, i0 + s, sa, ng, vp)
        return tuple(sa)

    lax.fori_loop(0, TB // SCAN_SETS, group, tuple(_scan_sa(sets[0], s_ref, ng, vp)))

    def emit(j, carry):
        for u in range(SCAN_EMIT_UNROLL):
            _scan_emit(yraw_ref, yf_ref, yb_ref, SCAN_EMIT_UNROLL * j + u, ng, vp)
        return carry

    lax.fori_loop(0, TB // SCAN_EMIT_UNROLL, emit, 0)

    @pl.when(tb == pl.num_programs(0) - 1)
    def _():
        sfin_ref[...] = s_ref[...]


def _scan_call(kk, r, v, w, bb, kx, s0):
    t, rows, _ = kk.shape
    ng, _, vp, _ = s0.shape
    nt = t // TB
    f1 = pl.BlockSpec((TB, rows, LANES), lambda ti: (ti, 0, 0))
    b1 = pl.BlockSpec((TB, rows, LANES), lambda ti: (nt - 1 - ti, 0, 0))
    f2 = pl.BlockSpec((1, TB, rows, LANES), lambda ti: (0, ti, 0, 0))
    b2 = pl.BlockSpec((1, TB, rows, LANES), lambda ti: (1, nt - 1 - ti, 0, 0))
    sspec = pl.BlockSpec((ng, HEAD, vp, LANES), lambda ti: (0, 0, 0, 0))
    tiles_per_set = 6 if ng == 2 else 3
    scratch = [pltpu.VMEM((ng, HEAD, vp, LANES), F32), pltpu.VMEM((TB, ng * vp, LANES), F32)]
    scratch += [pltpu.VMEM((LANES, LANES), F32)] * (SCAN_SETS * tiles_per_set)
    ysd = jax.ShapeDtypeStruct((t, rows, LANES), F32)
    return pl.pallas_call(
        functools.partial(_scan_kernel, ng=ng, vp=vp),
        grid=(nt,),
        in_specs=[f1, b1, f1, b1, f1, b1, f2, b2, f2, b2, f2, b2, sspec],
        out_specs=[f1, b1, sspec],
        out_shape=[ysd, ysd, jax.ShapeDtypeStruct((ng, HEAD, vp, LANES), F32)],
        scratch_shapes=scratch,
        compiler_params=_cparams("arbitrary"),
        name="wkv_scan",
    )(kk, kk, r, r, v, v, w, w, bb, bb, kx, kx, s0)


def _pool_seq_kernel(z_ref, a_ref, inv_ref, pw_ref, ps_ref, o_ref):
    for gi in range(len(POOL_WINDOWS)):
        sl = slice(gi * POOL_GROUP, (gi + 1) * POOL_GROUP)
        zg = z_ref[:, sl]
        s = _split_dot_left(a_ref[gi], zg)
        d = s * inv_ref[:, sl] - zg
        o_ref[:, sl] = _dot(d.astype(BF16), pw_ref[gi]) * ps_ref[:, sl]


def _pool_seq_call(z, t, amat, inv, pw, ps):
    n = z.shape[0]
    ng = len(POOL_WINDOWS)
    return pl.pallas_call(
        _pool_seq_kernel,
        grid=(n // t,),
        in_specs=[
            pl.BlockSpec((t, D_RWKV), lambda i: (i, 0)),
            pl.BlockSpec((ng, t, t), lambda i: (0, 0, 0)),
            pl.BlockSpec((t, D_RWKV), lambda i: (0, 0)),
            pl.BlockSpec((ng, POOL_GROUP, POOL_GROUP), lambda i: (0, 0, 0)),
            pl.BlockSpec((1, D_RWKV), lambda i: (0, 0)),
        ],
        out_specs=pl.BlockSpec((t, D_RWKV), lambda i: (i, 0)),
        out_shape=jax.ShapeDtypeStruct((n, D_RWKV), F32),
        compiler_params=_cparams("arbitrary"),
        name="pool_seq",
    )(z, amat, inv, pw, ps)


POOL_BLK = 256
POOL_PAD = (max(POOL_WINDOWS) // 2) * GRID_W


def _pool_grid_kernel(z_ref, a_ref, inv_ref, pw_ref, ps_ref, o_ref, pad_ref, *, t):
    zeros = jnp.zeros((POOL_PAD, POOL_GROUP), F32)
    pad_ref[0:POOL_PAD, :] = zeros
    pad_ref[POOL_PAD + t:POOL_PAD + t + POOL_PAD, :] = zeros
    for gi, win in enumerate(POOL_WINDOWS):
        sl = slice(gi * POOL_GROUP, (gi + 1) * POOL_GROUP)
        for blk in range(t // POOL_BLK):
            zb = z_ref[blk * POOL_BLK:(blk + 1) * POOL_BLK, sl]
            pad_ref[POOL_PAD + blk * POOL_BLK:POOL_PAD + (blk + 1) * POOL_BLK, :] = (
                _split_dot_left(a_ref[gi], zb))
        lo = win // 2
        hi = win - 1 - lo
        s = pad_ref[POOL_PAD - lo * GRID_W:POOL_PAD - lo * GRID_W + t, :]
        for j in range(-lo + 1, hi + 1):
            s = s + pad_ref[POOL_PAD + j * GRID_W:POOL_PAD + j * GRID_W + t, :]
        d = s * inv_ref[:, sl] - z_ref[:, sl]
        o_ref[:, sl] = _dot(d.astype(BF16), pw_ref[gi]) * ps_ref[:, sl]


def _pool_grid_call(z, t, amat, inv, pw, ps):
    n = z.shape[0]
    ng = len(POOL_WINDOWS)
    return pl.pallas_call(
        functools.partial(_pool_grid_kernel, t=t),
        grid=(n // t,),
        in_specs=[
            pl.BlockSpec((t, D_RWKV), lambda i: (i, 0)),
            pl.BlockSpec((ng, POOL_BLK, POOL_BLK), lambda i: (0, 0, 0)),
            pl.BlockSpec((t, D_RWKV), lambda i: (0, 0)),
            pl.BlockSpec((ng, POOL_GROUP, POOL_GROUP), lambda i: (0, 0, 0)),
            pl.BlockSpec((1, D_RWKV), lambda i: (0, 0)),
        ],
        out_specs=pl.BlockSpec((t, D_RWKV), lambda i: (i, 0)),
        out_shape=jax.ShapeDtypeStruct((n, D_RWKV), F32),
        scratch_shapes=[pltpu.VMEM((t + 2 * POOL_PAD, POOL_GROUP), F32)],
        compiler_params=_cparams("arbitrary"),
        name="pool_grid",
    )(z, amat, inv, pw, ps)


def _band(n, win):
    lo = win // 2
    hi = win - 1 - lo
    pos = np.arange(n)
    start = np.clip(pos - lo, 0, n)
    end = np.clip(pos + hi + 1, 0, n)
    m = (pos[None, :] >= start[:, None]) & (pos[None, :] < end[:, None])
    return m.astype(np.float32), (end - start).astype(np.float32)


def _pool_constants(t, use_grid):
    mats, invs = [], []
    for win in POOL_WINDOWS:
        if use_grid:
            rows = t // GRID_W
            mcol, ccol = _band(GRID_W, win)
            _, crow = _band(rows, win)
            mats.append(np.kron(np.eye(POOL_BLK // GRID_W, dtype=np.float32), mcol))
            cnt = (crow[:, None] * ccol[None, :]).reshape(t)
        else:
            m, cnt = _band(t, win)
            mats.append(m)
        invs.append(np.repeat((1.0 / cnt)[:, None], POOL_GROUP, axis=1))
    return (jnp.asarray(np.stack(mats), BF16),
            jnp.asarray(np.concatenate(invs, axis=1), F32))


def _post_kernel(x_ref, yf_ref, yb_ref, r_ref, v_ref, g_ref, kx_ref, yp_ref, mod_ref, lnw_ref,
                 lnb_ref, rk_ref, bd_ref, wout_ref, g2_ref, wrh_ref, wrl_ref,
                 x1_o, h2_o, pt_o, *, wide):
    mod = mod_ref[0]
    gt1 = mod[:, 2 * D:3 * D]
    sh2 = mod[:, 3 * D:4 * D]
    sc2 = mod[:, 4 * D:5 * D]
    bd = bd_ref[...]
    inv_n = np.float32(1.0 / HEAD)

    for h in range(SUB):
        tok = slice(h * TM, (h + 1) * TM)
        y = _tm_load(yf_ref, (), h, wide) + _tm_load(yb_ref, (), h, wide)
        mu = _split_dot(y, bd) * inv_n
        yc = y - mu
        var = _split_dot(yc * yc, bd) * inv_n
        yn = yc * lax.rsqrt(var + GN_EPS) * lnw_ref[...] + lnb_ref[...]
        v = _tm_load(v_ref, (), h, wide)
        kxs = _tm_load(kx_ref, (0,), h, wide) + _tm_load(kx_ref, (1,), h, wide)
        bonus = _split_dot(_tm_load(r_ref, (), h, wide) * kxs * rk_ref[...], bd) * v
        yr = ((yn + bonus) * g_ref[tok, :]).astype(BF16)
        m = _dot(yr, wout_ref[0:D_RWKV, :]) + _dot(yp_ref[tok, :].astype(BF16), wout_ref[D_RWKV:D, :])
        x1 = x_ref[tok, :] + gt1 * m
        x1_o[tok, :] = x1
        hn = x1 * lax.rsqrt(jnp.mean(x1 * x1, axis=-1, keepdims=True) + RMS_EPS)
        h2 = (hn * g2_ref[...]) * (1.0 + sc2) + sh2
        h2_o[tok, :] = h2.astype(BF16)

        hi = h2.astype(BF16)
        lo = (h2 - hi.astype(F32)).astype(BF16)
        dn = (((1,), (1,)), ((), ()))
        logits = (lax.dot_general(wrh_ref[...], hi, dn, preferred_element_type=F32)
                  + lax.dot_general(wrh_ref[...], lo, dn, preferred_element_type=F32)
                  + lax.dot_general(wrl_ref[...], hi, dn, preferred_element_type=F32))
        mx = jnp.max(logits, axis=0, keepdims=True)
        e = jnp.exp(logits - mx)
        pt_o[:, tok] = e / jnp.sum(e, axis=0, keepdims=True)


def _post_call(x, yf, yb, r, v, g, kx, yp, mod3, mod_row, t, lnw, lnb, rk, bd, wout, g2, wrh, wrl):
    n = x.shape[0]
    tg = SUB * TM
    const = lambda i: (0, 0)
    tok = lambda i: (i, 0)
    t512 = pl.BlockSpec((tg, D_RWKV), tok)
    v512 = pl.BlockSpec((1, D_RWKV), const)
    ps1 = _tm_spec(t, 0)
    return pl.pallas_call(
        functools.partial(_post_kernel, wide=(t == TM)),
        grid=(n // tg,),
        in_specs=[
            pl.BlockSpec((tg, D), tok), ps1, ps1, ps1, ps1, t512,
            _tm_spec(t, 1),
            t512,
            pl.BlockSpec((1, 1, N_MOD * D), lambda i: (mod_row(i), 0, 0)),
            v512, v512, v512,
            pl.BlockSpec((D_RWKV, D_RWKV), const),
            pl.BlockSpec((D, D), const),
            pl.BlockSpec((1, D), const),
            pl.BlockSpec((N_EXPERTS, D), const),
            pl.BlockSpec((N_EXPERTS, D), const),
        ],
        out_specs=[
            pl.BlockSpec((tg, D), tok),
            pl.BlockSpec((tg, D), tok),
            pl.BlockSpec((N_EXPERTS, tg), lambda i: (0, i)),
        ],
        out_shape=[jax.ShapeDtypeStruct((n, D), F32),
                   jax.ShapeDtypeStruct((n, D), BF16),
                   jax.ShapeDtypeStruct((N_EXPERTS, n), F32)],
        compiler_params=_cparams("arbitrary"),
        name="post_mix",
    )(x, yf, yb, r, v, g, kx, yp, mod3, lnw, lnb, rk, bd, wout, g2, wrh, wrl)


MOE_TF = 512


def _moe_kernel(xc_ref, xs_ref, gc_ref, gs_ref, bs_ref, mod_ref, w1_ref, w3_ref, w2_ref, oc_ref, os_ref,
                *, n_batch):
    j = pl.program_id(1)
    w1 = w1_ref[0, 0].astype(BF16)
    w3 = w3_ref[0, 0].astype(BF16)
    w2 = w2_ref[0, 0].astype(BF16)

    def ffn(x):
        h1 = _dot(x, w1)
        h3 = _dot(x, w3)
        return _dot((h1 * _sigmoid(h1) * h3).astype(BF16), w2)

    pc = ffn(xc_ref[0])
    ps = ffn(xs_ref[0])

    @pl.when(j == 0)
    def _():
        oc_ref[0] = pc
        os_ref[0] = ps

    @pl.when(j > 0)
    def _():
        oc_ref[0] = oc_ref[0] + pc
        os_ref[0] = os_ref[0] + ps

    @pl.when(j == pl.num_programs(1) - 1)
    def _():
        gt2 = mod_ref[0]
        oc_ref[0] = oc_ref[0] * gc_ref[0] * gt2[0:1, :]
        b = bs_ref[0]
        scale = gt2[n_batch:n_batch + 1, :]
        for q in range(n_batch - 1, 0, -1):
            scale = jnp.where(b == q - 1, gt2[q:q + 1, :], scale)
        os_ref[0] = os_ref[0] * gs_ref[0] * scale


def _moe_call(xc, xs, gc, gs, bidx, mod_all, w1, w3, w2, layer, n_batch):
    e, cc, _ = xc.shape
    cs = xs.shape[1]
    per_expert = lambda c, d: pl.BlockSpec((1, c, d), lambda ei, j: (ei, 0, 0))
    return pl.pallas_call(
        functools.partial(_moe_kernel, n_batch=n_batch),
        grid=(e, D_EXPERT // MOE_TF),
        in_specs=[
            per_expert(cc, D), per_expert(cs, D), per_expert(cc, 1), per_expert(cs, 1), per_expert(cs, 1),
            pl.BlockSpec((1, 8, D), lambda ei, j: (layer, 0, N_MOD - 1)),
            pl.BlockSpec((1, 1, D, MOE_TF), lambda ei, j: (layer, ei, 0, j)),
            pl.BlockSpec((1, 1, D, MOE_TF), lambda ei, j: (layer, ei, 0, j)),
            pl.BlockSpec((1, 1, MOE_TF, D), lambda ei, j: (layer, ei, j, 0)),
        ],
        out_specs=[per_expert(cc, D), per_expert(cs, D)],
        out_shape=[jax.ShapeDtypeStruct((e, cc, D), F32), jax.ShapeDtypeStruct((e, cs, D), F32)],
        compiler_params=_cparams("arbitrary", "arbitrary"),
        name="moe_ffn",
    )(xc, xs, gc, gs, bidx, mod_all, w1, w3, w2)


def _final_kernel(x_ref, g_ref, o_ref):
    x = x_ref[...]
    o_ref[...] = x * lax.rsqrt(jnp.mean(x * x, axis=-1, keepdims=True) + RMS_EPS) * g_ref[...]


def _final_call(x, gain):
    n = x.shape[0]
    tm = 512
    return pl.pallas_call(
        _final_kernel,
        grid=(n // tm,),
        in_specs=[pl.BlockSpec((tm, D), lambda i: (i, 0)), pl.BlockSpec((1, D), lambda i: (0, 0))],
        out_specs=pl.BlockSpec((tm, D), lambda i: (i, 0)),
        out_shape=jax.ShapeDtypeStruct((n, D), F32),
        compiler_params=_cparams("arbitrary"),
        name="final_norm",
    )(x, gain)


def _scan_pass(r, v, kkn, w, kx, bb, s0):
    t, width = r.shape
    rows = width // LANES
    flat = lambda u: u.reshape(u.shape[:-1] + (rows, LANES))
    yf, yb, sfin = _scan_call(flat(kkn), flat(r), flat(v), flat(w), flat(bb), flat(kx), s0)
    return yf.reshape(t, width), yb.reshape(t, width), sfin


def _state_to_lanes(s):
    b = s.shape[0]
    hv = HEAD // 2
    return (s.reshape(b, 2, N_PAIRS, 2, 2, hv, HEAD).transpose(6, 5, 4, 3, 1, 0, 2)
            .reshape(1, HEAD, hv, LANES))


def _state_from_lanes(s, b):
    return (s.reshape(2, HEAD, HEAD, 2, b, N_PAIRS).transpose(4, 3, 5, 0, 2, 1)
            .reshape(b, 2, N_HEADS, HEAD, HEAD))


def kernel(x_prompt, x_sample, state_rwkv, c, c_ctx, norm1, norm2, norm_f, ada_w, ada_b, w_in, mu_shift, w0, w_up, a0, a_up, g_up, k_k, k_a, r_k, ln_x_w, ln_x_b, pool_w, pool_scale, w_out, w_router, w1, w3, w2):
    bc, tc, _ = x_prompt.shape
    bs, ts, _ = x_sample.shape
    nc = bc * tc
    ns = bs * ts
    cap_c = 2 * nc // N_EXPERTS
    cap_s = 2 * ns // N_EXPERTS

    cv = jnp.concatenate([c_ctx[None, :], c, jnp.zeros((8 - 1 - bs, D), F32)], axis=0)
    mod_all = _mod_call(cv, ada_w, ada_b)

    win_b = w_in.astype(BF16)
    wout_b = w_out.astype(BF16)
    gup_b = g_up.astype(BF16)
    pw_b = pool_w.astype(BF16)
    zeros_up = jnp.zeros((DEPTH, HEAD, D_RWKV), F32)

    def blockdiag(u):
        top = jnp.concatenate([u[:, 0], zeros_up], axis=-1)
        bot = jnp.concatenate([zeros_up, u[:, 1]], axis=-1)
        return jnp.concatenate([top, bot], axis=1).astype(BF16)

    wup_b = blockdiag(w_up)
    aup_b = blockdiag(a_up)
    w0_f = w0.reshape(DEPTH, 1, 2 * D_RWKV)
    a0_f = a0.reshape(DEPTH, 1, 2 * D_RWKV)
    wr_t = jnp.swapaxes(w_router, 1, 2)
    wr_hi = wr_t.astype(BF16)
    wr_lo = (wr_t - wr_hi.astype(F32)).astype(BF16)
    bd = jnp.asarray(np.kron(np.eye(N_HEADS, dtype=np.float32), np.ones((HEAD, HEAD), np.float32)), BF16)

    amat_c, inv_c = _pool_constants(tc, False)
    amat_s, inv_s = _pool_constants(ts, True)

    row_c = lambda i: 0
    steps_s = ts // (SUB * TM)
    row_s = lambda i: 1 + i // steps_s

    xc = x_prompt.reshape(nc, D)
    xs = x_sample.reshape(ns, D)
    ctx_states = []
    zero_state = jnp.zeros((2, HEAD, HEAD, LANES), F32)
    for l in range(DEPTH):
        mod3 = mod_all[l].reshape(8, 1, N_MOD * D)
        row1 = lambda u: u[l].reshape(1, -1)
        lay_in = (row1(norm1), win_b[l], row1(mu_shift), w0_f[l], wup_b[l], a0_f[l], aup_b[l], gup_b[l],
                  row1(k_k), row1(k_a), bd)
        lay_post = (row1(ln_x_w), row1(ln_x_b), row1(r_k), bd, wout_b[l], row1(norm2), wr_hi[l], wr_lo[l])

        r, v, kkn, g, w, kx, bb, zp = _in_call(xc, mod3, row_c, tc, *lay_in)
        yf, yb, sfin = _scan_pass(r, v, kkn, w, kx, bb, zero_state)
        ctx_states.append(_state_from_lanes(sfin, bc))
        yp = _pool_seq_call(zp, tc, amat_c, inv_c, pw_b[l], row1(pool_scale))
        x1c, h2c, ptc = _post_call(xc, yf, yb, r, v, g, kx, yp, mod3, row_c, tc, *lay_post)

        r, v, kkn, g, w, kx, bb, zp = _in_call(xs, mod3, row_s, ts, *lay_in)
        yf, yb, _ = _scan_pass(r, v, kkn, w, kx, bb, _state_to_lanes(state_rwkv[:, l]))
        yp = _pool_grid_call(zp, ts, amat_s, inv_s, pw_b[l], row1(pool_scale))
        x1s, h2s, pts = _post_call(xs, yf, yb, r, v, g, kx, yp, mod3, row_s, ts, *lay_post)

        gate_c, idx_c = lax.top_k(ptc, cap_c)
        gate_s, idx_s = lax.top_k(pts, cap_s)
        bidx = (idx_s // ts).astype(jnp.int32)[..., None]
        ye_c, ye_s = _moe_call(h2c[idx_c], h2s[idx_s], gate_c[..., None], gate_s[..., None], bidx, mod_all,
                               w1, w3, w2, l, bs)
        xc = x1c.at[idx_c.reshape(-1)].add(ye_c.reshape(-1, D))
        xs = x1s.at[idx_s.reshape(-1)].add(ye_s.reshape(-1, D))

    gf = norm_f.reshape(1, D)
    y_prompt = _final_call(xc, gf).reshape(bc, tc, D)
    y_sample = _final_call(xs, gf).reshape(bs, ts, D)
    new_state = jnp.stack(ctx_states, axis=1)
    return (y_prompt, y_sample, new_state)
```
